```python
import math
import jax, jax.numpy as jnp
from jax import lax
import numpy as np

D_MODEL = 1024
BATCH = 8
SEQ = 8192
DEPTH = 4

GRID_W = 64
D_FF = 2816
N_AB = (DEPTH + 1) // 2
N_C = DEPTH // 2
D_POOL = D_MODEL // 2
POOL_WINDOWS = (2, 4, 8, 16)
N_POOL_GROUPS = len(POOL_WINDOWS)
POOL_GROUP = D_POOL // N_POOL_GROUPS
D_SSM = D_MODEL - D_POOL
SSM_GROUP = 16
N_SSM_GROUPS = D_SSM // SSM_GROUP
SSM_STATE = 64
N_HEADS = 16
HEAD_DIM = D_MODEL // N_HEADS
D_ATTN = N_HEADS * HEAD_DIM
MAX_KH = 8
KW = 16
RMS_EPS = 1e-6
DT_MIN = 1e-3
DT_MAX = 1e-1
A_RE_MAX = -1e-4

kernel_name = 'hybrid_pool_s5_natten_macaron'

F32 = jnp.float32


def rms_norm(x, g):
    xf = x.astype(F32)
    y = xf * lax.rsqrt(jnp.mean(xf * xf, axis=-1, keepdims=True) + RMS_EPS)
    return (y * g.astype(F32)).astype(x.dtype)


def swiglu_ffn(x, w_gate, w_up, w_down):
    return (jax.nn.silu(x @ w_gate) * (x @ w_up)) @ w_down


def pool_mixer(u, w_grp, scale):
    bsz, L, _ = u.shape
    uf = u.astype(F32)
    cs = jnp.concatenate([jnp.zeros((bsz, 1, D_POOL), F32), jnp.cumsum(uf, axis=1)], axis=1)
    t = jnp.arange(L)
    outs = []
    for gi, w in enumerate(POOL_WINDOWS):
        lo = w // 2
        hi = w - 1 - lo
        start = jnp.clip(t - lo, 0, L)
        end = jnp.clip(t + hi + 1, 0, L)
        c0, c1 = gi * POOL_GROUP, (gi + 1) * POOL_GROUP
        csg = cs[:, :, c0:c1]
        cnt = (end - start).astype(F32)[None, :, None]
        outs.append((csg[:, end] - csg[:, start]) / cnt - uf[:, :, c0:c1])
    p = jnp.stack(outs, axis=2)
    y = jnp.einsum('blgc,gcd->blgd', p, w_grp.astype(F32)).reshape(bsz, L, D_POOL)
    return (y * scale.astype(F32)).astype(u.dtype)


def ssm_scan_dir(u, a_re, a_im, log_dt, b_re, b_im, c_re, c_im):
    L = u.shape[0]
    a_re = jnp.minimum(a_re.astype(F32), A_RE_MAX)
    a_im = a_im.astype(F32)
    dt = jnp.exp(log_dt.astype(F32))[:, None]
    mag = jnp.exp(a_re * dt)
    lam_re = mag * jnp.cos(a_im * dt)
    lam_im = mag * jnp.sin(a_im * dt)
    num_re = lam_re - 1.0
    num_im = lam_im
    den = a_re * a_re + a_im * a_im
    f_re = ((num_re * a_re + num_im * a_im) / den)[..., None]
    f_im = ((num_im * a_re - num_re * a_im) / den)[..., None]
    b_re = b_re.astype(F32)
    b_im = b_im.astype(F32)
    bb_re = f_re * b_re - f_im * b_im
    bb_im = f_re * b_im + f_im * b_re
    x_re = jnp.einsum('lbgh,gph->lbgp', u, bb_re)
    x_im = jnp.einsum('lbgh,gph->lbgp', u, bb_im)
    shp = (L, 1) + lam_re.shape
    l_re = jnp.broadcast_to(lam_re[None, None], shp)
    l_im = jnp.broadcast_to(lam_im[None, None], shp)

    def combine(e1, e2):
        a1r, a1i, b1r, b1i = e1
        a2r, a2i, b2r, b2i = e2
        return (a2r * a1r - a2i * a1i,
                a2r * a1i + a2i * a1r,
                a2r * b1r - a2i * b1i + b2r,
                a2r * b1i + a2i * b1r + b2i)

    _, _, s_re, s_im = lax.associative_scan(combine, (l_re, l_im, x_re, x_im), axis=0)
    return (jnp.einsum('lbgp,ghp->lbgh', s_re, c_re.astype(F32))
            - jnp.einsum('lbgp,ghp->lbgh', s_im, c_im.astype(F32)))


def s5_mixer(u, a_re, a_im, log_dt, b_re, b_im, c_re, c_im, d_skip, w_glu, b_glu):
    bsz, L, _ = u.shape
    uf = u.astype(F32)
    ug = jnp.transpose(uf.reshape(bsz, L, N_SSM_GROUPS, SSM_GROUP), (1, 0, 2, 3))
    y_f = ssm_scan_dir(ug, a_re[0], a_im[0], log_dt[0], b_re[0], b_im[0], c_re[0], c_im[0])
    y_b = jnp.flip(ssm_scan_dir(jnp.flip(ug, axis=0), a_re[1], a_im[1], log_dt[1],
                                b_re[1], b_im[1], c_re[1], c_im[1]), axis=0)
    y = jnp.transpose(y_f + y_b, (1, 0, 2, 3)).reshape(bsz, L, D_SSM) + d_skip.astype(F32) * uf
    y = jax.nn.gelu(y)
    y = y * jax.nn.sigmoid(y @ w_glu.astype(F32) + b_glu.astype(F32))
    return y.astype(u.dtype)


def neighborhood_attention(h, w_qkv, rpb, w_out):
    bsz, L, _ = h.shape
    rows = L // GRID_W
    kh = min(MAX_KH, rows)
    qkv = (h @ w_qkv).reshape(bsz, rows, GRID_W, 3, N_HEADS, HEAD_DIM)
    q = qkv[:, :, :, 0] * (HEAD_DIM ** -0.5)
    k = qkv[:, :, :, 1]
    v = qkv[:, :, :, 2]
    col = jnp.arange(GRID_W)
    col_start = jnp.clip(col - KW // 2, 0, GRID_W - KW)
    col_idx = col_start[:, None] + jnp.arange(KW)[None, :]
    col_bias_idx = col_idx - col[:, None] + (KW - 1)
    rpb_c = rpb.astype(F32)[:, :, col_bias_idx]

    def row_step(r):
        r0 = jnp.clip(r - kh // 2, 0, rows - kh)
        q_r = lax.dynamic_index_in_dim(q, r, axis=1, keepdims=False)
        k_band = lax.dynamic_slice_in_dim(k, r0, kh, axis=1)
        v_band = lax.dynamic_slice_in_dim(v, r0, kh, axis=1)
        k_nb = k_band[:, :, col_idx]
        v_nb = v_band[:, :, col_idx]
        row_bias_idx = r0 + jnp.arange(kh) - r + (MAX_KH - 1)
        bias = jnp.transpose(jnp.take(rpb_c, row_bias_idx, axis=1), (0, 2, 1, 3))
        s = jnp.einsum('bqhd,biqjhd->bhqij', q_r, k_nb).astype(F32) + bias[None]
        p = jax.nn.softmax(s.reshape(bsz, N_HEADS, GRID_W, kh * KW), axis=-1)
        p = p.reshape(bsz, N_HEADS, GRID_W, kh, KW).astype(v.dtype)
        return jnp.einsum('bhqij,biqjhd->bqhd', p, v_nb)

    o = lax.map(row_step, jnp.arange(rows))
    o = jnp.transpose(o, (1, 0, 2, 3, 4)).reshape(bsz, L, D_ATTN)
    return o @ w_out


def setup_inputs(seed: int = 0) -> dict:
    key = jax.random.key(seed)
    ks = jax.random.split(key, 24)
    nrm = jax.random.normal
    G, P, H = N_SSM_GROUPS, SSM_STATE, SSM_GROUP
    x = nrm(ks[0], (BATCH, SEQ, D_MODEL), F32)
    norm_g = 1.0 + 0.02 * nrm(ks[1], (DEPTH, 6, D_MODEL), F32)
    ffn_w_gate = nrm(ks[2], (DEPTH, 2, D_MODEL, D_FF), F32) * D_MODEL ** -0.5
    ffn_w_up = nrm(ks[3], (DEPTH, 2, D_MODEL, D_FF), F32) * D_MODEL ** -0.5
    ffn_w_down = nrm(ks[4], (DEPTH, 2, D_FF, D_MODEL), F32) * D_FF ** -0.5
    ab_w_in = nrm(ks[5], (N_AB, D_MODEL, D_POOL + D_SSM), F32) * D_MODEL ** -0.5
    pool_w = nrm(ks[6], (N_AB, N_POOL_GROUPS, POOL_GROUP, POOL_GROUP), F32) * POOL_GROUP ** -0.5
    pool_scale = 1.0 + 0.02 * nrm(ks[7], (N_AB, D_POOL), F32)
    n_idx = jnp.arange(P, dtype=F32)
    ssm_A_re = -0.5 + 0.01 * nrm(ks[8], (N_AB, 2, G, P), F32)
    ssm_A_im = math.pi * n_idx + 0.01 * nrm(ks[9], (N_AB, 2, G, P), F32)
    ssm_log_dt = jax.random.uniform(ks[10], (N_AB, 2, G), F32, math.log(DT_MIN), math.log(DT_MAX))
    ssm_B_re = nrm(ks[11], (N_AB, 2, G, P, H), F32) * (2.0 * H) ** -0.5
    ssm_B_im = nrm(ks[12], (N_AB, 2, G, P, H), F32) * (2.0 * H) ** -0.5
    ssm_C_re = nrm(ks[13], (N_AB, 2, G, H, P), F32) * (2.0 * P) ** -0.5
    ssm_C_im = nrm(ks[14], (N_AB, 2, G, H, P), F32) * (2.0 * P) ** -0.5
    ssm_D = nrm(ks[15], (N_AB, D_SSM), F32)
    ssm_w_glu = nrm(ks[16], (N_AB, D_SSM, D_SSM), F32) * D_SSM ** -0.5
    ssm_b_glu = 0.01 * nrm(ks[17], (N_AB, D_SSM), F32)
    ab_w_out = nrm(ks[18], (N_AB, D_POOL + D_SSM, D_MODEL), F32) * (D_POOL + D_SSM) ** -0.5
    na_w_qkv = nrm(ks[19], (N_C, D_MODEL, 3 * D_ATTN), F32) * D_MODEL ** -0.5
    na_rpb = 0.02 * nrm(ks[20], (N_C, N_HEADS, 2 * MAX_KH - 1, 2 * KW - 1), F32)
    na_w_out = nrm(ks[21], (N_C, D_ATTN, D_MODEL), F32) * D_ATTN ** -0.5
    return {'x': x, 'norm_g': norm_g, 'ffn_w_gate': ffn_w_gate, 'ffn_w_up': ffn_w_up,
            'ffn_w_down': ffn_w_down, 'ab_w_in': ab_w_in, 'pool_w': pool_w,
            'pool_scale': pool_scale, 'ssm_A_re': ssm_A_re, 'ssm_A_im': ssm_A_im,
            'ssm_log_dt': ssm_log_dt, 'ssm_B_re': ssm_B_re, 'ssm_B_im': ssm_B_im,
            'ssm_C_re': ssm_C_re, 'ssm_C_im': ssm_C_im, 'ssm_D': ssm_D,
            'ssm_w_glu': ssm_w_glu, 'ssm_b_glu': ssm_b_glu, 'ab_w_out': ab_w_out,
            'na_w_qkv': na_w_qkv, 'na_rpb': na_rpb, 'na_w_out': na_w_out}


def reference(x, norm_g, ffn_w_gate, ffn_w_up, ffn_w_down, ab_w_in, pool_w, pool_scale,
              ssm_A_re, ssm_A_im, ssm_log_dt, ssm_B_re, ssm_B_im, ssm_C_re, ssm_C_im,
              ssm_D, ssm_w_glu, ssm_b_glu, ab_w_out, na_w_qkv, na_rpb, na_w_out):
    h = x
    for layer in range(DEPTH):
        g = norm_g[layer]
        f = swiglu_ffn(rms_norm(h, g[0]), ffn_w_gate[layer, 0], ffn_w_up[layer, 0], ffn_w_down[layer, 0])
        h = h + 0.5 * rms_norm(f, g[1])
        hn = rms_norm(h, g[2])
        i = layer // 2
        if layer % 2 == 0:
            z = hn @ ab_w_in[i]
            ya = pool_mixer(z[..., :D_POOL], pool_w[i], pool_scale[i])
            yb = s5_mixer(z[..., D_POOL:], ssm_A_re[i], ssm_A_im[i], ssm_log_dt[i],
                          ssm_B_re[i], ssm_B_im[i], ssm_C_re[i], ssm_C_im[i],
                          ssm_D[i], ssm_w_glu[i], ssm_b_glu[i])
            m = jnp.concatenate([ya, yb], axis=-1) @ ab_w_out[i]
        else:
            m = neighborhood_attention(hn, na_w_qkv[i], na_rpb[i], na_w_out[i])
        h = h + rms_norm(m, g[3])
        f = swiglu_ffn(rms_norm(h, g[4]), ffn_w_gate[layer, 1], ffn_w_up[layer, 1], ffn_w_down[layer, 1])
        h = h + 0.5 * rms_norm(f, g[5])
    return h
```

```python
import functools
import math

import jax
import jax.numpy as jnp
from jax import lax
from jax.experimental import pallas as pl
from jax.experimental.pallas import tpu as pltpu

F32 = jnp.float32
BF16 = jnp.bfloat16

RMS_EPS = 1e-6
A_RE_MAX = -1e-4
GRID_W = 64
POOL_WINDOWS = (2, 4, 8, 16)
SSM_GROUP = 16
SSM_STATE = 64
N_HEADS = 16
MAX_KH = 8
KW = 16
MASK_VALUE = -1e30

SUBLANES = 8
LANES = 128
VMEM_LIMIT_BYTES = 56 * 1024 * 1024

ROW_TILE = 512
SSM_HALF_GROUPS = 16


def _rms(x, g):
    return x * lax.rsqrt(jnp.mean(x * x, axis=-1, keepdims=True) + RMS_EPS) * g


def _dot(a, b):
    return jnp.dot(a, b, preferred_element_type=F32)


def _const_spec(shape):
    nd = len(shape)
    return pl.BlockSpec(shape, lambda *_: (0,) * nd, pipeline_mode=pl.Buffered(1))


def _params(n_axes):
    return pltpu.CompilerParams(
        dimension_semantics=("arbitrary",) * n_axes, vmem_limit_bytes=VMEM_LIMIT_BYTES)


def _ffn_body(x_ref, g_ref, wg_ref, wu_ref, wd_ref, o_ref):
    x = x_ref[...]
    xn = _rms(x, g_ref[0:1, :]).astype(BF16)
    gate = _dot(xn, wg_ref[...])
    up = _dot(xn, wu_ref[...])
    act = (gate * jax.nn.sigmoid(gate) * up).astype(BF16)
    f = _dot(act, wd_ref[...])
    o_ref[...] = x + 0.5 * _rms(f, g_ref[1:2, :])


def _ffn_call(x, g_pair, wg, wu, wd, *, mode, batch, seq):
    d = wg.shape[0]
    f = wg.shape[1]
    w_specs = [_const_spec((2, d)), _const_spec((d, f)), _const_spec((d, f)), _const_spec((f, d))]
    tl = ROW_TILE
    if mode == "mid":
        n = x.shape[0]
        grid = (n // ROW_TILE,)
        x_spec = pl.BlockSpec((ROW_TILE, d), lambda i: (i, 0))
        o_spec = pl.BlockSpec((ROW_TILE, d), lambda i: (i, 0))
        out_shape = jax.ShapeDtypeStruct((n, d), F32)
    elif mode == "first":
        grid = (seq // tl, batch)
        x_spec = pl.BlockSpec((None, tl, d), lambda i, b: (b, i, 0))
        o_spec = pl.BlockSpec((tl, d), lambda i, b: (i, b))
        out_shape = jax.ShapeDtypeStruct((seq, batch * d), F32)
    else:
        x = x.reshape(seq, batch * d)
        grid = (seq // tl, batch)
        x_spec = pl.BlockSpec((tl, d), lambda i, b: (i, b))
        o_spec = pl.BlockSpec((None, tl, d), lambda i, b: (b, i, 0))
        out_shape = jax.ShapeDtypeStruct((batch, seq, d), F32)
    out = pl.pallas_call(
        _ffn_body, grid=grid, in_specs=[x_spec] + w_specs, out_specs=o_spec,
        out_shape=out_shape, compiler_params=_params(len(grid)), name="ffn_" + mode,
    )(x, g_pair, wg, wu, wd)
    if mode == "first":
        out = out.reshape(seq * batch, d)
    return out


def _scan_half(x_scr, s_scr, lam_ref, st_ref, half, n_steps, reverse):
    w = x_scr.shape[1] // 2
    lr = lam_ref[half, 0]
    li = lam_ref[half, 1]

    def step(j, carry):
        sr, si = carry
        t = (n_steps - 1 - j) if reverse else j
        r0 = pl.multiple_of(t * SUBLANES, SUBLANES)
        xr = x_scr[pl.ds(r0, SUBLANES), 0:w]
        xi = x_scr[pl.ds(r0, SUBLANES), w:2 * w]
        nr = lr * sr - li * si + xr
        ni = lr * si + li * sr + xi
        s_scr[pl.ds(r0, SUBLANES), 0:w] = nr
        s_scr[pl.ds(r0, SUBLANES), w:2 * w] = ni
        return nr, ni

    sr, si = lax.fori_loop(0, n_steps, step, (st_ref[half, 0], st_ref[half, 1]))
    st_ref[half, 0] = sr
    st_ref[half, 1] = si


def _ssm_dir(u_bf, b_ref, lam_ref, c_ref, st_ref, x_scr, s_scr, reverse):
    n_half = b_ref.shape[0]
    kh = b_ref.shape[1]
    n_steps = x_scr.shape[0] // SUBLANES
    ys = []
    for half in range(n_half):
        x_scr[...] = _dot(u_bf[:, half * kh:(half + 1) * kh], b_ref[half])
        _scan_half(x_scr, s_scr, lam_ref, st_ref, half, n_steps, reverse)
        ys.append(_dot(s_scr[...].astype(BF16), c_ref[half]))
    return jnp.concatenate(ys, axis=-1)


def _s5_fwd_body(h_ref, g_ref, win_ref, b_ref, lam_ref, c_ref, z_ref, y_ref, st_ref, x_scr, s_scr):
    @pl.when(pl.program_id(0) == 0)
    def _():
        st_ref[...] = jnp.zeros_like(st_ref)

    d_pool = z_ref.shape[1] - y_ref.shape[1]
    hn = _rms(h_ref[...], g_ref[...]).astype(BF16)
    z = _dot(hn, win_ref[...])
    z_ref[...] = z
    u_bf = z[:, d_pool:].astype(BF16)
    y_ref[...] = _ssm_dir(u_bf, b_ref, lam_ref, c_ref, st_ref, x_scr, s_scr, reverse=False)


def _s5_bwd_body(h_ref, z_ref, zp_ref, zn_ref, yf_ref, g_ref, b_ref, lam_ref, c_ref, dsk_ref,
                 wglu_ref, bglu_ref, pw_ref, ps_ref, wout_ref, o_ref, st_ref, x_scr, s_scr, pad_scr,
                 *, seq, batch):
    i = pl.program_id(0)
    n_tiles = pl.num_programs(0)
    it = n_tiles - 1 - i

    @pl.when(i == 0)
    def _():
        st_ref[...] = jnp.zeros_like(st_ref)

    rows = z_ref.shape[0]
    d_pool = pad_scr.shape[1]
    halo = zp_ref.shape[0]
    tl = rows // batch
    z = z_ref[...]

    u = z[:, d_pool:]
    y = yf_ref[...] + _ssm_dir(u.astype(BF16), b_ref, lam_ref, c_ref, st_ref, x_scr, s_scr, reverse=True)
    y = y + dsk_ref[...] * u
    y = jax.nn.gelu(y, approximate=True)
    yb = y * jax.nn.sigmoid(_dot(y.astype(BF16), wglu_ref[...]) + bglu_ref[...])

    zero_halo = jnp.zeros((halo, d_pool), F32)
    pad_scr[0:halo, :] = jnp.where(it == 0, zero_halo, zp_ref[...])
    pad_scr[halo:halo + rows, :] = z[:, :d_pool]
    pad_scr[halo + rows:, :] = jnp.where(it == n_tiles - 1, zero_halo, zn_ref[...])
    n_grp = len(POOL_WINDOWS)
    cg = d_pool // n_grp
    t_glob = it * tl + lax.broadcasted_iota(jnp.int32, (rows, cg), 0) // batch
    outs = []
    for gi, wnd in enumerate(POOL_WINDOWS):
        lo = wnd // 2
        hi = wnd - 1 - lo
        c0, c1 = gi * cg, (gi + 1) * cg
        acc = pad_scr[halo - lo * batch:halo - lo * batch + rows, c0:c1]
        for j in range(-lo + 1, hi + 1):
            acc = acc + pad_scr[halo + j * batch:halo + j * batch + rows, c0:c1]
        cnt = (jnp.clip(t_glob + hi + 1, 0, seq) - jnp.clip(t_glob - lo, 0, seq)).astype(F32)
        p = acc / cnt - z[:, c0:c1]
        outs.append(_dot(p.astype(BF16), pw_ref[gi]))
    ya = jnp.concatenate(outs, axis=-1) * ps_ref[...]

    m = _dot(jnp.concatenate([ya, yb], axis=-1).astype(BF16), wout_ref[...])
    o_ref[...] = h_ref[...] + _rms(m, g_ref[...])


def _ssm_discretise(a_re, a_im, log_dt, b_re, b_im, c_re, c_im):
    g, p = a_re.shape
    hch = b_re.shape[-1]
    a_re = jnp.minimum(a_re.astype(F32), A_RE_MAX)
    a_im = a_im.astype(F32)
    dt = jnp.exp(log_dt.astype(F32))[:, None]
    mag = jnp.exp(a_re * dt)
    lam_re = mag * jnp.cos(a_im * dt)
    lam_im = mag * jnp.sin(a_im * dt)
    num_re = lam_re - 1.0
    num_im = lam_im
    den = a_re * a_re + a_im * a_im
    f_re = ((num_re * a_re + num_im * a_im) / den)[..., None]
    f_im = ((num_im * a_re - num_re * a_im) / den)[..., None]
    b_re = b_re.astype(F32)
    b_im = b_im.astype(F32)
    bb_re = f_re * b_re - f_im * b_im
    bb_im = f_re * b_im + f_im * b_re
    gh = SSM_HALF_GROUPS
    n_half = g // gh
    w = gh * p
    eye = jnp.eye(gh, dtype=F32)

    def blockdiag_in(bb):
        bb = bb.reshape(n_half, gh, p, hch)
        return jnp.einsum("ngph,gk->nghkp", bb, eye).reshape(n_half, gh * hch, w)

    def blockdiag_out(c):
        c = c.reshape(n_half, gh, hch, p)
        return jnp.einsum("nghp,gk->ngpkh", c, eye).reshape(n_half, w, gh * hch)

    bmat = jnp.concatenate([blockdiag_in(bb_re), blockdiag_in(bb_im)], axis=-1).astype(BF16)
    cmat = jnp.concatenate([blockdiag_out(c_re.astype(F32)), -blockdiag_out(c_im.astype(F32))], axis=1).astype(BF16)
    lam = jnp.stack([lam_re.reshape(n_half, w), lam_im.reshape(n_half, w)], axis=1)
    lam = jnp.broadcast_to(lam[:, :, None, :], (n_half, 2, SUBLANES, w))
    return lam, bmat, cmat


def _mixer_even(h, g_in, g_out, w_in, pool_w, pool_scale, a_re, a_im, log_dt, b_re, b_im, c_re, c_im,
                d_skip, w_glu, b_glu, w_out, *, batch, seq):
    n, d = h.shape
    d_pool = pool_w.shape[0] * pool_w.shape[1]
    d_ssm = d - d_pool
    rows = ROW_TILE
    n_tiles = n // rows
    lam_f, bm_f, cm_f = _ssm_discretise(a_re[0], a_im[0], log_dt[0], b_re[0], b_im[0], c_re[0], c_im[0])
    lam_b, bm_b, cm_b = _ssm_discretise(a_re[1], a_im[1], log_dt[1], b_re[1], b_im[1], c_re[1], c_im[1])
    n_half, kh, w2 = bm_f.shape
    st_shape = pltpu.VMEM((n_half, 2, SUBLANES, w2 // 2), F32)
    scan_scratch = [st_shape, pltpu.VMEM((rows, w2), F32), pltpu.VMEM((rows, w2), F32)]
    ssm_specs = [_const_spec(bm_f.shape), _const_spec(lam_f.shape), _const_spec(cm_f.shape)]

    z, y_f = pl.pallas_call(
        _s5_fwd_body, grid=(n_tiles,),
        in_specs=[pl.BlockSpec((rows, d), lambda i: (i, 0)), _const_spec((1, d)), _const_spec((d, d))] + ssm_specs,
        out_specs=[pl.BlockSpec((rows, d), lambda i: (i, 0)), pl.BlockSpec((rows, d_ssm), lambda i: (i, 0))],
        out_shape=[jax.ShapeDtypeStruct((n, d), F32), jax.ShapeDtypeStruct((n, d_ssm), F32)],
        scratch_shapes=scan_scratch, compiler_params=_params(1), name="s5_fwd",
    )(h, g_in.reshape(1, d), w_in.astype(BF16), bm_f, lam_f, cm_f)

    halo = max(POOL_WINDOWS) // 2 * batch
    hb = rows // halo
    n_hblk = n // halo
    rev = lambda i: (n_tiles - 1 - i, 0)
    prev_map = lambda i: (jnp.maximum((n_tiles - 1 - i) * hb - 1, 0), 0)
    next_map = lambda i: (jnp.minimum((n_tiles - i) * hb, n_hblk - 1), 0)
    body = functools.partial(_s5_bwd_body, seq=seq, batch=batch)
    return pl.pallas_call(
        body, grid=(n_tiles,),
        in_specs=[pl.BlockSpec((rows, d), rev), pl.BlockSpec((rows, d), rev),
                  pl.BlockSpec((halo, d_pool), prev_map), pl.BlockSpec((halo, d_pool), next_map),
                  pl.BlockSpec((rows, d_ssm), rev), _const_spec((1, d))] + ssm_specs +
                 [_const_spec((1, d_ssm)), _const_spec((d_ssm, d_ssm)), _const_spec((1, d_ssm)),
                  _const_spec(pool_w.shape), _const_spec((1, d_pool)), _const_spec((d, d))],
        out_specs=pl.BlockSpec((rows, d), rev),
        out_shape=jax.ShapeDtypeStruct((n, d), F32),
        scratch_shapes=scan_scratch + [pltpu.VMEM((rows + 2 * halo, d_pool), F32)],
        compiler_params=_params(1), name="s5_bwd_mix",
    )(h, z, z, z, y_f, g_out.reshape(1, d), bm_b, lam_b, cm_b, d_skip.reshape(1, d_ssm),
      w_glu.astype(BF16), b_glu.reshape(1, d_ssm), pool_w.astype(BF16), pool_scale.reshape(1, d_pool),
      w_out.astype(BF16))


def _qkv_body(h_ref, g_ref, w_ref, o_ref, *, d_attn, q_scale):
    hn = _rms(h_ref[...], g_ref[...]).astype(BF16)
    qkv = _dot(hn, w_ref[...])
    o_ref[:, :d_attn] = (qkv[:, :d_attn] * q_scale).astype(BF16)
    o_ref[:, d_attn:] = qkv[:, d_attn:].astype(BF16)


def _na_body(q_ref, k_ref, v_ref, bias_ref, o_ref, *, n_rows, kh, head_dim):
    gw = GRID_W
    lane = lax.broadcasted_iota(jnp.int32, (gw, 2 * head_dim), 1)
    first = lane < head_dim

    def row(r, _):
        r0 = jnp.clip(r - kh // 2, 0, n_rows - kh)
        off = r0 - r + (MAX_KH - 1)
        q = q_ref[pl.ds(pl.multiple_of(r * gw, gw), gw), :]
        zero = jnp.zeros_like(q)
        q2 = jnp.concatenate([jnp.where(first, q, zero), jnp.where(first, zero, q)], axis=0)
        kstart = pl.multiple_of(r0 * gw, gw)
        kb = k_ref[pl.ds(kstart, kh * gw), :]
        vb = v_ref[pl.ds(kstart, kh * gw), :]
        s = lax.dot_general(q2, kb, (((1,), (1,)), ((), ())), preferred_element_type=F32)
        s = s + bias_ref[off]
        m = jnp.max(s, axis=-1, keepdims=True)
        e = jnp.exp(s - m)
        l = jnp.sum(e, axis=-1, keepdims=True)
        o2 = _dot(e.astype(BF16), vb) / l
        o = jnp.where(first, o2[:gw], o2[gw:])
        o_ref[pl.ds(pl.multiple_of(r * gw, gw), gw), :] = o.astype(o_ref.dtype)
        return 0

    lax.fori_loop(0, n_rows, row, 0)


def _na_out_body(a_ref, h_ref, g_ref, w_ref, o_ref):
    o_ref[...] = h_ref[...] + _rms(_dot(a_ref[...], w_ref[...]), g_ref[...])


def _na_bias_tables(rpb, kh):
    n_heads = rpb.shape[0]
    col = jnp.arange(GRID_W)
    col_start = jnp.clip(col - KW // 2, 0, GRID_W - KW)
    kc = jnp.arange(GRID_W)
    rel = kc[None, :] - col[:, None] + (KW - 1)
    valid = (kc[None, :] >= col_start[:, None]) & (kc[None, :] < col_start[:, None] + KW)
    rel = jnp.clip(rel, 0, 2 * KW - 2)
    t = rpb.astype(F32)[:, :, rel]
    t = jnp.where(valid[None, None], t, MASK_VALUE)
    offs = jnp.arange(MAX_KH)[:, None] + jnp.arange(kh)[None, :]
    offs = jnp.clip(offs, 0, 2 * MAX_KH - 2)
    tab = t[:, offs]
    tab = jnp.transpose(tab, (0, 1, 3, 2, 4)).reshape(n_heads, MAX_KH, GRID_W, kh * GRID_W)
    tab = tab.reshape(n_heads // 2, 2, MAX_KH, GRID_W, kh * GRID_W)
    return jnp.transpose(tab, (0, 2, 1, 3, 4)).reshape(n_heads // 2, MAX_KH, 2 * GRID_W, kh * GRID_W)


def _mixer_odd(h, g_in, g_out, w_qkv, rpb, w_out, *, batch, seq):
    n, d = h.shape
    d_attn = w_qkv.shape[1] // 3
    head_dim = d_attn // N_HEADS
    rows = ROW_TILE
    n_tiles = n // rows
    n_rows = seq // GRID_W
    kh = min(MAX_KH, n_rows)
    hp_lanes = 2 * head_dim
    n_hp = d_attn // hp_lanes

    qkv = pl.pallas_call(
        functools.partial(_qkv_body, d_attn=d_attn, q_scale=head_dim ** -0.5), grid=(n_tiles,),
        in_specs=[pl.BlockSpec((rows, d), lambda i: (i, 0)), _const_spec((1, d)), _const_spec((d, 3 * d_attn))],
        out_specs=pl.BlockSpec((rows, 3 * d_attn), lambda i: (i, 0)),
        out_shape=jax.ShapeDtypeStruct((n, 3 * d_attn), BF16),
        compiler_params=_params(1), name="na_qkv",
    )(h, g_in.reshape(1, d), w_qkv.astype(BF16))

    qkv = qkv.reshape(seq, batch * 3 * d_attn)
    bias = _na_bias_tables(rpb, kh)
    cb = 3 * n_hp
    attn = pl.pallas_call(
        functools.partial(_na_body, n_rows=n_rows, kh=kh, head_dim=head_dim), grid=(n_hp, batch),
        in_specs=[pl.BlockSpec((seq, hp_lanes), lambda p, b: (0, b * cb + p)),
                  pl.BlockSpec((seq, hp_lanes), lambda p, b: (0, b * cb + n_hp + p)),
                  pl.BlockSpec((seq, hp_lanes), lambda p, b: (0, b * cb + 2 * n_hp + p)),
                  pl.BlockSpec((None, MAX_KH, 2 * GRID_W, kh * GRID_W), lambda p, b: (p, 0, 0, 0))],
        out_specs=pl.BlockSpec((seq, hp_lanes), lambda p, b: (0, b * n_hp + p)),
        out_shape=jax.ShapeDtypeStruct((seq, batch * d_attn), BF16),
        compiler_params=_params(2), name="na_attn",
    )(qkv, qkv, qkv, bias)
    attn = attn.reshape(n, d_attn)

    return pl.pallas_call(
        _na_out_body, grid=(n_tiles,),
        in_specs=[pl.BlockSpec((rows, d_attn), lambda i: (i, 0)), pl.BlockSpec((rows, d), lambda i: (i, 0)),
                  _const_spec((1, d)), _const_spec((d_attn, d))],
        out_specs=pl.BlockSpec((rows, d), lambda i: (i, 0)),
        out_shape=jax.ShapeDtypeStruct((n, d), F32),
        compiler_params=_params(1), name="na_out",
    )(attn, h, g_out.reshape(1, d), w_out.astype(BF16))


def kernel(x, norm_g, ffn_w_gate, ffn_w_up, ffn_w_down, ab_w_in, pool_w, pool_scale, ssm_A_re, ssm_A_im, ssm_log_dt, ssm_B_re, ssm_B_im, ssm_C_re, ssm_C_im, ssm_D, ssm_w_glu, ssm_b_glu, ab_w_out, na_w_qkv, na_rpb, na_w_out):
    batch, seq, d = x.shape
    depth = norm_g.shape[0]
    assert batch == SUBLANES and seq % ROW_TILE == 0 and (seq * batch) % ROW_TILE == 0

    def ffn(h, layer, which, mode):
        g = norm_g[layer]
        return _ffn_call(h, g[0:2] if which == 0 else g[4:6],
                         ffn_w_gate[layer, which].astype(BF16), ffn_w_up[layer, which].astype(BF16),
                         ffn_w_down[layer, which].astype(BF16), mode=mode, batch=batch, seq=seq)

    h = x
    for layer in range(depth):
        g = norm_g[layer]
        h = ffn(h, layer, 0, "first" if layer == 0 else "mid")
        i = layer // 2
        if layer % 2 == 0:
            h = _mixer_even(h, g[2], g[3], ab_w_in[i], pool_w[i], pool_scale[i], ssm_A_re[i], ssm_A_im[i],
                            ssm_log_dt[i], ssm_B_re[i], ssm_B_im[i], ssm_C_re[i], ssm_C_im[i], ssm_D[i],
                            ssm_w_glu[i], ssm_b_glu[i], ab_w_out[i], batch=batch, seq=seq)
        else:
            h = _mixer_odd(h, g[2], g[3], na_w_qkv[i], na_rpb[i], na_w_out[i], batch=batch, seq=seq)
        h = ffn(h, layer, 1, "last" if layer == depth - 1 else "mid")
    return h
```

```python
import functools

import jax
import jax.numpy as jnp
from jax import lax
from jax.experimental import pallas as pl
from jax.experimental.pallas import tpu as pltpu

F32 = jnp.float32
BF16 = jnp.bfloat16

RMS_EPS = 1e-6
A_RE_MAX = -1e-4
GRID_W = 64
POOL_WINDOWS = (2, 4, 8, 16)
N_HEADS = 16
MAX_KH = 8
KW = 16
MASK_VALUE = -1e30

SUBLANES = 8
LANES = 128
BF16_SUBLANES = 16
VMEM_LIMIT_BYTES = 56 * 1024 * 1024

ROW_TILE = 512
SSM_HALF_GROUPS = 16
NA_ROW_UNROLL = 8


def _rms(x, g):
    return x * lax.rsqrt(jnp.mean(x * x, axis=-1, keepdims=True) + RMS_EPS) * g


def _dot(a, b):
    return jnp.dot(a, b, preferred_element_type=F32)


def _const_spec(shape):
    nd = len(shape)
    return pl.BlockSpec(shape, lambda *_: (0,) * nd, pipeline_mode=pl.Buffered(1))


def _params(n_axes):
    return pltpu.CompilerParams(
        dimension_semantics=("arbitrary",) * n_axes, vmem_limit_bytes=VMEM_LIMIT_BYTES)


def _row_spec(width):
    return pl.BlockSpec((ROW_TILE, width), lambda i: (i, 0))


def _ffn_residual(x, g_ref, wg_ref, wu_ref, wd_ref):
    xn = _rms(x, g_ref[0:1, :]).astype(BF16)
    gate = _dot(xn, wg_ref[...])
    up = _dot(xn, wu_ref[...])
    act = (gate * jax.nn.sigmoid(gate) * up).astype(BF16)
    f = _dot(act, wd_ref[...])
    return x + 0.5 * _rms(f, g_ref[1:2, :])


def _ffn_body(x_ref, g_ref, wg_ref, wu_ref, wd_ref, o_ref):
    o_ref[...] = _ffn_residual(x_ref[...], g_ref, wg_ref, wu_ref, wd_ref)


def _attn_out_ffn_body(a_ref, x_ref, go_ref, wo_ref, g_ref, wg_ref, wu_ref, wd_ref, o_ref):
    x = x_ref[...] + _rms(_dot(a_ref[...], wo_ref[...]), go_ref[...])
    o_ref[...] = _ffn_residual(x, g_ref, wg_ref, wu_ref, wd_ref)


def _ffn_call(x, g_pair, wg, wu, wd, attn=None, g_attn=None, w_attn=None):
    n, d = x.shape
    f = wg.shape[1]
    ffn_specs = [_const_spec((2, d)), _const_spec((d, f)), _const_spec((d, f)), _const_spec((f, d))]
    if attn is None:
        body, name, pre_specs, pre_args = _ffn_body, "ffn", [_row_spec(d)], (x,)
    else:
        da = attn.shape[1]
        body, name = _attn_out_ffn_body, "attn_out_ffn"
        pre_specs = [_row_spec(da), _row_spec(d), _const_spec((1, d)), _const_spec((da, d))]
        pre_args = (attn, x, g_attn.reshape(1, d), w_attn)
    return pl.pallas_call(
        body, grid=(n // ROW_TILE,), in_specs=pre_specs + ffn_specs,
        out_specs=_row_spec(d), out_shape=jax.ShapeDtypeStruct((n, d), F32),
        compiler_params=_params(1), name=name,
    )(*pre_args, g_pair, wg, wu, wd)


def _to_time_major(src_ref, slab_scr):
    nb, tl, d = src_ref.shape
    for b in range(nb):
        for s in range(d // LANES):
            slab_scr[s, pl.ds(b, tl, stride=nb), :] = src_ref[b, :, s * LANES:(s + 1) * LANES]
    return jnp.concatenate([slab_scr[s] for s in range(d // LANES)], axis=-1)


def _add_from_time_major(val, res_ref, dst_ref, slab_scr):
    nb, tl, d = dst_ref.shape
    for s in range(d // LANES):
        slab_scr[s] = val[:, s * LANES:(s + 1) * LANES]
    for b in range(nb):
        for s in range(d // LANES):
            cols = slice(s * LANES, (s + 1) * LANES)
            dst_ref[b, :, cols] = res_ref[b, :, cols] + slab_scr[s, pl.ds(b, tl, stride=nb), :]


def _scan_half(x_scr, s_scr, lam_ref, st_ref, half, reverse):
    rows, w2 = x_scr.shape[1], x_scr.shape[2]
    w = w2 // 2
    n_steps = rows // SUBLANES
    lr = lam_ref[half, 0]
    li = lam_ref[half, 1]
    sr = st_ref[half, 0]
    si = st_ref[half, 1]
    order = range(n_steps - 1, -1, -1) if reverse else range(n_steps)
    held = None
    for t in order:
        r0 = t * SUBLANES
        xr = x_scr[half, r0:r0 + SUBLANES, 0:w]
        xi = x_scr[half, r0:r0 + SUBLANES, w:w2]
        sr, si = lr * sr - li * si + xr, lr * si + li * sr + xi
        if held is None:
            held = (sr, si)
        else:
            pr, pi = held
            lo = min(t, t + (1 if reverse else -1)) * SUBLANES
            pair_r = (sr, pr) if reverse else (pr, sr)
            pair_i = (si, pi) if reverse else (pi, si)
            s_scr[half, lo:lo + BF16_SUBLANES, 0:w] = jnp.concatenate(pair_r, axis=0).astype(BF16)
            s_scr[half, lo:lo + BF16_SUBLANES, w:w2] = jnp.concatenate(pair_i, axis=0).astype(BF16)
            held = None
    st_ref[half, 0] = sr
    st_ref[half, 1] = si


def _ssm_dir(u_bf, b_ref, lam_ref, c_ref, st_ref, x_scr, s_scr, reverse):
    n_half, kh = b_ref.shape[0], b_ref.shape[1]
    ys = []
    for half in range(n_half):
        x_scr[half] = _dot(u_bf[:, half * kh:(half + 1) * kh], b_ref[half])
        _scan_half(x_scr, s_scr, lam_ref, st_ref, half, reverse)
        ys.append(_dot(s_scr[half], c_ref[half]))
    return jnp.concatenate(ys, axis=-1)


def _s5_fwd_body(h_ref, g_ref, win_ref, b_ref, lam_ref, c_ref, z_ref, y_ref, st_ref, x_scr, s_scr, slab_scr):
    @pl.when(pl.program_id(0) == 0)
    def _():
        st_ref[...] = jnp.zeros_like(st_ref)

    d_pool = z_ref.shape[1] - y_ref.shape[1]
    hn = _rms(_to_time_major(h_ref, slab_scr), g_ref[...]).astype(BF16)
    z = _dot(hn, win_ref[...])
    z_ref[...] = z
    u_bf = z[:, d_pool:].astype(BF16)
    y_ref[...] = _ssm_dir(u_bf, b_ref, lam_ref, c_ref, st_ref, x_scr, s_scr, reverse=False)


def _s5_bwd_body(h_ref, z_ref, zp_ref, zn_ref, yf_ref, g_ref, b_ref, lam_ref, c_ref, dsk_ref,
                 wglu_ref, bglu_ref, pw_ref, ps_ref, wout_ref, o_ref, st_ref, x_scr, s_scr, slab_scr, pad_scr,
                 *, seq):
    i = pl.program_id(0)
    n_tiles = pl.num_programs(0)
    it = n_tiles - 1 - i

    @pl.when(i == 0)
    def _():
        st_ref[...] = jnp.zeros_like(st_ref)

    batch = h_ref.shape[0]
    rows = z_ref.shape[0]
    d_pool = pad_scr.shape[1]
    halo = zp_ref.shape[0]
    tl = rows // batch
    z = z_ref[...]

    u = z[:, d_pool:]
    y = yf_ref[...] + _ssm_dir(u.astype(BF16), b_ref, lam_ref, c_ref, st_ref, x_scr, s_scr, reverse=True)
    y = y + dsk_ref[...] * u
    y = jax.nn.gelu(y, approximate=True)
    yb = y * jax.nn.sigmoid(_dot(y.astype(BF16), wglu_ref[...]) + bglu_ref[...])

    zero_halo = jnp.zeros((halo, d_pool), F32)
    pad_scr[0:halo, :] = jnp.where(it == 0, zero_halo, zp_ref[...])
    pad_scr[halo:halo + rows, :] = z[:, :d_pool]
    pad_scr[halo + rows:, :] = jnp.where(it == n_tiles - 1, zero_halo, zn_ref[...])
    n_grp = len(POOL_WINDOWS)
    cg = d_pool // n_grp
    t_glob = it * tl + lax.broadcasted_iota(jnp.int32, (rows, cg), 0) // batch
    outs = []
    for gi, wnd in enumerate(POOL_WINDOWS):
        lo = wnd // 2
        hi = wnd - 1 - lo
        c0, c1 = gi * cg, (gi + 1) * cg
        acc = pad_scr[halo - lo * batch:halo - lo * batch + rows, c0:c1]
        for j in range(-lo + 1, hi + 1):
            acc = acc + pad_scr[halo + j * batch:halo + j * batch + rows, c0:c1]
        cnt = (jnp.clip(t_glob + hi + 1, 0, seq) - jnp.clip(t_glob - lo, 0, seq)).astype(F32)
        p = acc / cnt - z[:, c0:c1]
        outs.append(_dot(p.astype(BF16), pw_ref[gi]))
    ya = jnp.concatenate(outs, axis=-1) * ps_ref[...]

    m = _dot(jnp.concatenate([ya, yb], axis=-1).astype(BF16), wout_ref[...])
    _add_from_time_major(_rms(m, g_ref[...]), h_ref, o_ref, slab_scr)


def _ssm_discretise(a_re, a_im, log_dt, b_re, b_im, c_re, c_im):
    g, p = a_re.shape
    hch = b_re.shape[-1]
    a_re = jnp.minimum(a_re.astype(F32), A_RE_MAX)
    a_im = a_im.astype(F32)
    dt = jnp.exp(log_dt.astype(F32))[:, None]
    mag = jnp.exp(a_re * dt)
    lam_re = mag * jnp.cos(a_im * dt)
    lam_im = mag * jnp.sin(a_im * dt)
    num_re = lam_re - 1.0
    num_im = lam_im
    den = a_re * a_re + a_im * a_im
    f_re = ((num_re * a_re + num_im * a_im) / den)[..., None]
    f_im = ((num_im * a_re - num_re * a_im) / den)[..., None]
    b_re = b_re.astype(F32)
    b_im = b_im.astype(F32)
    bb_re = f_re * b_re - f_im * b_im
    bb_im = f_re * b_im + f_im * b_re
    gh = SSM_HALF_GROUPS
    n_half = g // gh
    w = gh * p
    eye = jnp.eye(gh, dtype=F32)

    def blockdiag_in(bb):
        bb = bb.reshape(n_half, gh, p, hch)
        return jnp.einsum("ngph,gk->nghkp", bb, eye).reshape(n_half, gh * hch, w)

    def blockdiag_out(c):
        c = c.reshape(n_half, gh, hch, p)
        return jnp.einsum("nghp,gk->ngpkh", c, eye).reshape(n_half, w, gh * hch)

    bmat = jnp.concatenate([blockdiag_in(bb_re), blockdiag_in(bb_im)], axis=-1).astype(BF16)
    cmat = jnp.concatenate([blockdiag_out(c_re.astype(F32)), -blockdiag_out(c_im.astype(F32))], axis=1).astype(BF16)
    lam = jnp.stack([lam_re.reshape(n_half, w), lam_im.reshape(n_half, w)], axis=1)
    lam = jnp.broadcast_to(lam[:, :, None, :], (n_half, 2, SUBLANES, w))
    return lam, bmat, cmat


def _mixer_even(h, g_in, g_out, w_in, pool_w, pool_scale, a_re, a_im, log_dt, b_re, b_im, c_re, c_im,
                d_skip, w_glu, b_glu, w_out, *, batch, seq):
    n, d = h.shape
    d_pool = pool_w.shape[0] * pool_w.shape[1]
    d_ssm = d - d_pool
    rows = ROW_TILE
    tl = rows // batch
    n_tiles = n // rows
    h3 = h.reshape(batch, seq, d)
    lam_f, bm_f, cm_f = _ssm_discretise(a_re[0], a_im[0], log_dt[0], b_re[0], b_im[0], c_re[0], c_im[0])
    lam_b, bm_b, cm_b = _ssm_discretise(a_re[1], a_im[1], log_dt[1], b_re[1], b_im[1], c_re[1], c_im[1])
    n_half, kh, w2 = bm_f.shape
    scan_scratch = [pltpu.VMEM((n_half, 2, SUBLANES, w2 // 2), F32), pltpu.VMEM((n_half, rows, w2), F32),
                    pltpu.VMEM((n_half, rows, w2), BF16), pltpu.VMEM((d // LANES, rows, LANES), F32)]
    ssm_specs = [_const_spec(bm_f.shape), _const_spec(lam_f.shape), _const_spec(cm_f.shape)]

    z, y_f = pl.pallas_call(
        _s5_fwd_body, grid=(n_tiles,),
        in_specs=[pl.BlockSpec((batch, tl, d), lambda i: (0, i, 0)), _const_spec((1, d)), _const_spec((d, d))]
                 + ssm_specs,
        out_specs=[_row_spec(d), _row_spec(d_ssm)],
        out_shape=[jax.ShapeDtypeStruct((n, d), F32), jax.ShapeDtypeStruct((n, d_ssm), F32)],
        scratch_shapes=scan_scratch, compiler_params=_params(1), name="s5_fwd",
    )(h3, g_in.reshape(1, d), w_in.astype(BF16), bm_f, lam_f, cm_f)

    halo = max(POOL_WINDOWS) // 2 * batch
    hb = rows // halo
    n_hblk = n // halo
    rev = lambda i: (n_tiles - 1 - i, 0)
    rev3 = lambda i: (0, n_tiles - 1 - i, 0)
    prev_map = lambda i: (jnp.maximum((n_tiles - 1 - i) * hb - 1, 0), 0)
    next_map = lambda i: (jnp.minimum((n_tiles - i) * hb, n_hblk - 1), 0)
    out = pl.pallas_call(
        functools.partial(_s5_bwd_body, seq=seq), grid=(n_tiles,),
        in_specs=[pl.BlockSpec((batch, tl, d), rev3), pl.BlockSpec((rows, d), rev),
                  pl.BlockSpec((halo, d_pool), prev_map), pl.BlockSpec((halo, d_pool), next_map),
                  pl.BlockSpec((rows, d_ssm), rev), _const_spec((1, d))] + ssm_specs +
                 [_const_spec((1, d_ssm)), _const_spec((d_ssm, d_ssm)), _const_spec((1, d_ssm)),
                  _const_spec(pool_w.shape), _const_spec((1, d_pool)), _const_spec((d, d))],
        out_specs=pl.BlockSpec((batch, tl, d), rev3),
        out_shape=jax.ShapeDtypeStruct((batch, seq, d), F32),
        scratch_shapes=scan_scratch + [pltpu.VMEM((rows + 2 * halo, d_pool), F32)],
        compiler_params=_params(1), name="s5_bwd_mix",
    )(h3, z, z, z, y_f, g_out.reshape(1, d), bm_b, lam_b, cm_b, d_skip.reshape(1, d_ssm),
      w_glu.astype(BF16), b_glu.reshape(1, d_ssm), pool_w.astype(BF16), pool_scale.reshape(1, d_pool),
      w_out.astype(BF16))
    return out.reshape(n, d)


def _qkv_body(h_ref, g_ref, w_ref, o_ref, *, d_attn, q_scale):
    hn = _rms(h_ref[...], g_ref[...]).astype(BF16)
    qkv = _dot(hn, w_ref[...])
    o_ref[:, :d_attn] = (qkv[:, :d_attn] * q_scale).astype(BF16)
    o_ref[:, d_attn:] = qkv[:, d_attn:].astype(BF16)


def _na_body(q_ref, k_ref, v_ref, bias_ref, o_ref, *, n_rows, kh, head_dim):
    gw = GRID_W
    lane = lax.broadcasted_iota(jnp.int32, (gw, 2 * head_dim), 1)
    first = lane < head_dim

    def step(j, _):
        rows, scores, probs = [], [], []
        for k in range(NA_ROW_UNROLL):
            r = j * NA_ROW_UNROLL + k
            r0 = jnp.clip(r - kh // 2, 0, n_rows - kh)
            qrows = pl.ds(pl.multiple_of(r * gw, gw), gw)
            krows = pl.ds(pl.multiple_of(r0 * gw, gw), kh * gw)
            q = q_ref[qrows, :]
            zero = jnp.zeros_like(q)
            q2 = jnp.concatenate([jnp.where(first, q, zero), jnp.where(first, zero, q)], axis=0)
            s = lax.dot_general(q2, k_ref[krows, :], (((1,), (1,)), ((), ())), preferred_element_type=F32)
            rows.append((qrows, krows))
            scores.append(s + bias_ref[r0 - r + (MAX_KH - 1)])
        for s in scores:
            e = jnp.exp(s - jnp.max(s, axis=-1, keepdims=True))
            probs.append((e.astype(BF16), jnp.sum(e, axis=-1, keepdims=True)))
        for (qrows, krows), (p, l) in zip(rows, probs):
            o2 = _dot(p, v_ref[krows, :]) / l
            o_ref[qrows, :] = jnp.where(first, o2[:gw], o2[gw:]).astype(o_ref.dtype)
        return 0

    lax.fori_loop(0, n_rows // NA_ROW_UNROLL, step, 0)


def _na_bias_tables(rpb, kh):
    n_heads = rpb.shape[0]
    col = jnp.arange(GRID_W)
    col_start = jnp.clip(col - KW // 2, 0, GRID_W - KW)
    kc = jnp.arange(GRID_W)
    rel = kc[None, :] - col[:, None] + (KW - 1)
    valid = (kc[None, :] >= col_start[:, None]) & (kc[None, :] < col_start[:, None] + KW)
    rel = jnp.clip(rel, 0, 2 * KW - 2)
    t = rpb.astype(F32)[:, :, rel]
    t = jnp.where(valid[None, None], t, MASK_VALUE)
    offs = jnp.arange(MAX_KH)[:, None] + jnp.arange(kh)[None, :]
    tab = t[:, offs]
    tab = jnp.transpose(tab, (0, 1, 3, 2, 4)).reshape(n_heads, MAX_KH, GRID_W, kh * GRID_W)
    tab = tab.reshape(n_heads // 2, 2, MAX_KH, GRID_W, kh * GRID_W)
    return jnp.transpose(tab, (0, 2, 1, 3, 4)).reshape(n_heads // 2, MAX_KH, 2 * GRID_W, kh * GRID_W)


def _mixer_odd(h, g_in, w_qkv, rpb, *, batch, seq):
    n, d = h.shape
    d_attn = w_qkv.shape[1] // 3
    head_dim = d_attn // N_HEADS
    n_tiles = n // ROW_TILE
    n_rows = seq // GRID_W
    kh = min(MAX_KH, n_rows)
    hp_lanes = 2 * head_dim
    n_hp = d_attn // hp_lanes

    qkv = pl.pallas_call(
        functools.partial(_qkv_body, d_attn=d_attn, q_scale=head_dim ** -0.5), grid=(n_tiles,),
        in_specs=[_row_spec(d), _const_spec((1, d)), _const_spec((d, 3 * d_attn))],
        out_specs=_row_spec(3 * d_attn), out_shape=jax.ShapeDtypeStruct((n, 3 * d_attn), BF16),
        compiler_params=_params(1), name="na_qkv",
    )(h, g_in.reshape(1, d), w_qkv.astype(BF16))

    bias = _na_bias_tables(rpb, kh)
    attn = pl.pallas_call(
        functools.partial(_na_body, n_rows=n_rows, kh=kh, head_dim=head_dim), grid=(n_hp, batch),
        in_specs=[pl.BlockSpec((seq, hp_lanes), lambda p, b: (b, p)),
                  pl.BlockSpec((seq, hp_lanes), lambda p, b: (b, n_hp + p)),
                  pl.BlockSpec((seq, hp_lanes), lambda p, b: (b, 2 * n_hp + p)),
                  pl.BlockSpec((None, MAX_KH, 2 * GRID_W, kh * GRID_W), lambda p, b: (p, 0, 0, 0))],
        out_specs=pl.BlockSpec((seq, hp_lanes), lambda p, b: (b, p)),
        out_shape=jax.ShapeDtypeStruct((n, d_attn), BF16),
        compiler_params=_params(2), name="na_attn",
    )(qkv, qkv, qkv, bias)
    return attn


def kernel(x, norm_g, ffn_w_gate, ffn_w_up, ffn_w_down, ab_w_in, pool_w, pool_scale, ssm_A_re, ssm_A_im, ssm_log_dt, ssm_B_re, ssm_B_im, ssm_C_re, ssm_C_im, ssm_D, ssm_w_glu, ssm_b_glu, ab_w_out, na_w_qkv, na_rpb, na_w_out):
    batch, seq, d = x.shape
    depth = norm_g.shape[0]
    n_rows = seq // GRID_W
    assert batch == SUBLANES and seq % ROW_TILE == 0 and n_rows % NA_ROW_UNROLL == 0

    def ffn(h, layer, which, **attn_args):
        g = norm_g[layer]
        return _ffn_call(h, g[0:2] if which == 0 else g[4:6], ffn_w_gate[layer, which].astype(BF16),
                         ffn_w_up[layer, which].astype(BF16), ffn_w_down[layer, which].astype(BF16), **attn_args)

    h = x.reshape(batch * seq, d)
    for layer in range(depth):
        g = norm_g[layer]
        h = ffn(h, layer, 0)
        i = layer // 2
        if layer % 2 == 0:
            h = _mixer_even(h, g[2], g[3], ab_w_in[i], pool_w[i], pool_scale[i], ssm_A_re[i], ssm_A_im[i],
                            ssm_log_dt[i], ssm_B_re[i], ssm_B_im[i], ssm_C_re[i], ssm_C_im[i], ssm_D[i],
                            ssm_w_glu[i], ssm_b_glu[i], ab_w_out[i], batch=batch, seq=seq)
            h = ffn(h, layer, 1)
        else:
            attn = _mixer_odd(h, g[2], na_w_qkv[i], na_rpb[i], batch=batch, seq=seq)
            h = ffn(h, layer, 1, attn=attn, g_attn=g[3], w_attn=na_w_out[i].astype(BF16))
    return h.reshape(batch, seq, d)
```

```python
import functools

import jax
import jax.numpy as jnp
from jax import lax
from jax.experimental import pallas as pl
from jax.experimental.pallas import tpu as pltpu

F32 = jnp.float32
BF16 = jnp.bfloat16

RMS_EPS = 1e-6
A_RE_MAX = -1e-4
GRID_W = 64
POOL_WINDOWS = (2, 4, 8, 16)
N_HEADS = 16
MAX_KH = 8
KW = 16
MASK_VALUE = -1e30

SUBLANES = 8
LANES = 128
BF16_SUBLANES = 16
VMEM_LIMIT_BYTES = 56 * 1024 * 1024

ROW_TILE = 512
SSM_HALF_GROUPS = 16
NA_ROW_UNROLL = 8
OUT_CHUNK = 256
OUT_ROW_GROUPS = 2


def _rms(x, g):
    return x * lax.rsqrt(jnp.mean(x * x, axis=-1, keepdims=True) + RMS_EPS) * g


def _dot(a, b):
    return jnp.dot(a, b, preferred_element_type=F32)


def _const_spec(shape):
    nd = len(shape)
    return pl.BlockSpec(shape, lambda *_: (0,) * nd, pipeline_mode=pl.Buffered(1))


def _params(n_axes):
    return pltpu.CompilerParams(
        dimension_semantics=("arbitrary",) * n_axes, vmem_limit_bytes=VMEM_LIMIT_BYTES)


def _row_spec(width):
    return pl.BlockSpec((ROW_TILE, width), lambda i: (i, 0))


def _inv_rms(x):
    return lax.rsqrt(jnp.mean(x * x, axis=-1, keepdims=True) + RMS_EPS)


def _normed_dot(x, g, inv, w):
    return _dot((x * g).astype(BF16), w) * inv


def _ffn_residual(x, g_ref, wg_ref, wu_ref, wd_ref, o_ref):
    d = x.shape[1]
    xg = (x * g_ref[0:1, :]).astype(BF16)
    inv = _inv_rms(x)
    gate = _dot(xg, wg_ref[...]) * inv
    up = _dot(xg, wu_ref[...]) * inv
    act = (gate * jax.nn.sigmoid(gate) * up).astype(BF16)
    rg = x.shape[0] // OUT_ROW_GROUPS
    for r in range(OUT_ROW_GROUPS):
        rows = slice(r * rg, (r + 1) * rg)
        chunks, ssq = [], None
        for c in range(0, d, OUT_CHUNK):
            f = _dot(act[rows], wd_ref[:, c:c + OUT_CHUNK])
            chunks.append(f)
            sq = jnp.sum(f * f, axis=-1, keepdims=True)
            ssq = sq if ssq is None else ssq + sq
        half_inv = 0.5 * lax.rsqrt(ssq / d + RMS_EPS)
        for k, f in enumerate(chunks):
            cols = slice(k * OUT_CHUNK, (k + 1) * OUT_CHUNK)
            o_ref[rows, cols] = x[rows, cols] + f * half_inv * g_ref[1:2, cols]


def _ffn_body(x_ref, g_ref, wg_ref, wu_ref, wd_ref, o_ref):
    _ffn_residual(x_ref[...], g_ref, wg_ref, wu_ref, wd_ref, o_ref)


def _attn_out_ffn_body(a_ref, x_ref, go_ref, wo_ref, g_ref, wg_ref, wu_ref, wd_ref, o_ref):
    x = x_ref[...] + _rms(_dot(a_ref[...], wo_ref[...]), go_ref[...])
    _ffn_residual(x, g_ref, wg_ref, wu_ref, wd_ref, o_ref)


def _ffn_call(x, g_pair, wg, wu, wd, attn=None, g_attn=None, w_attn=None):
    n, d = x.shape
    f = wg.shape[1]
    ffn_specs = [_const_spec((2, d)), _const_spec((d, f)), _const_spec((d, f)), _const_spec((f, d))]
    if attn is None:
        body, name, pre_specs, pre_args = _ffn_body, "ffn", [_row_spec(d)], (x,)
    else:
        da = attn.shape[1]
        body, name = _attn_out_ffn_body, "attn_out_ffn"
        pre_specs = [_row_spec(da), _row_spec(d), _const_spec((1, d)), _const_spec((da, d))]
        pre_args = (attn, x, g_attn.reshape(1, d), w_attn)
    return pl.pallas_call(
        body, grid=(n // ROW_TILE,), in_specs=pre_specs + ffn_specs,
        out_specs=_row_spec(d), out_shape=jax.ShapeDtypeStruct((n, d), F32),
        compiler_params=_params(1), name=name,
    )(*pre_args, g_pair, wg, wu, wd)


def _to_time_major(src_ref, slab_scr):
    nb, tl, d = src_ref.shape
    for b in range(nb):
        for s in range(d // LANES):
            slab_scr[s, pl.ds(b, tl, stride=nb), :] = src_ref[b, :, s * LANES:(s + 1) * LANES]
    return jnp.concatenate([slab_scr[s] for s in range(d // LANES)], axis=-1)


def _add_from_time_major(val, res_ref, dst_ref, slab_scr):
    nb, tl, d = dst_ref.shape
    for s in range(d // LANES):
        slab_scr[s] = val[:, s * LANES:(s + 1) * LANES]
    for b in range(nb):
        for s in range(d // LANES):
            cols = slice(s * LANES, (s + 1) * LANES)
            dst_ref[b, :, cols] = res_ref[b, :, cols] + slab_scr[s, pl.ds(b, tl, stride=nb), :]


def _scan_half(x_scr, s_scr, lam_ref, st_ref, half, reverse):
    rows, w2 = x_scr.shape[1], x_scr.shape[2]
    w = w2 // 2
    n_steps = rows // SUBLANES
    lr = lam_ref[half, 0]
    li = lam_ref[half, 1]
    sr = st_ref[half, 0]
    si = st_ref[half, 1]
    order = range(n_steps - 1, -1, -1) if reverse else range(n_steps)
    held = None
    for t in order:
        r0 = t * SUBLANES
        xr = x_scr[half, r0:r0 + SUBLANES, 0:w]
        xi = x_scr[half, r0:r0 + SUBLANES, w:w2]
        sr, si = lr * sr - li * si + xr, lr * si + li * sr + xi
        if held is None:
            held = (sr, si)
        else:
            pr, pi = held
            lo = min(t, t + (1 if reverse else -1)) * SUBLANES
            pair_r = (sr, pr) if reverse else (pr, sr)
            pair_i = (si, pi) if reverse else (pi, si)
            s_scr[half, lo:lo + BF16_SUBLANES, 0:w] = jnp.concatenate(pair_r, axis=0).astype(BF16)
            s_scr[half, lo:lo + BF16_SUBLANES, w:w2] = jnp.concatenate(pair_i, axis=0).astype(BF16)
            held = None
    st_ref[half, 0] = sr
    st_ref[half, 1] = si


def _s5_in_body(h_ref, g_ref, win_ref, z_ref, slab_scr):
    h = _to_time_major(h_ref, slab_scr)
    z_ref[...] = _normed_dot(h, g_ref[...], _inv_rms(h), win_ref[...])


def _s5_scan_body(uf_ref, ub_ref, bf_ref, lamf_ref, cf_ref, bb_ref, lamb_ref, cb_ref, yf_ref, yb_ref,
                  st_ref, x_scr, s_scr):
    @pl.when(pl.program_id(0) == 0)
    def _():
        st_ref[...] = jnp.zeros_like(st_ref)

    dirs = ((uf_ref, bf_ref, lamf_ref, cf_ref, yf_ref, False), (ub_ref, bb_ref, lamb_ref, cb_ref, yb_ref, True))
    n_half, kh = bf_ref.shape[0], bf_ref.shape[1]
    for k, (u_ref, b_ref, _, _, _, _) in enumerate(dirs):
        u_bf = u_ref[...].astype(BF16)
        for half in range(n_half):
            x_scr[k, half] = _dot(u_bf[:, half * kh:(half + 1) * kh], b_ref[half])
    for k, (_, _, lam_ref, c_ref, y_ref, reverse) in enumerate(dirs):
        for half in range(n_half):
            _scan_half(x_scr.at[k], s_scr.at[k], lam_ref, st_ref.at[k], half, reverse)
            y_ref[:, half * kh:(half + 1) * kh] = _dot(s_scr[k, half], c_ref[half])


def _s5_post_body(h_ref, z_ref, zp_ref, zn_ref, yf_ref, yb_ref, g_ref, dsk_ref, wglu_ref, bglu_ref,
                  pw_ref, ps_ref, wout_ref, o_ref, slab_scr, pad_scr, *, seq):
    it = pl.program_id(0)
    n_tiles = pl.num_programs(0)
    batch = h_ref.shape[0]
    rows = z_ref.shape[0]
    d_pool = pad_scr.shape[1]
    halo = zp_ref.shape[0]
    tl = rows // batch
    z = z_ref[...]

    u = z[:, d_pool:]
    y = yf_ref[...] + yb_ref[...] + dsk_ref[...] * u
    y = jax.nn.gelu(y, approximate=True)
    yb = y * jax.nn.sigmoid(_dot(y.astype(BF16), wglu_ref[...]) + bglu_ref[...])

    zero_halo = jnp.zeros((halo, d_pool), F32)
    pad_scr[0:halo, :] = jnp.where(it == 0, zero_halo, zp_ref[...])
    pad_scr[halo:halo + rows, :] = z[:, :d_pool]
    pad_scr[halo + rows:, :] = jnp.where(it == n_tiles - 1, zero_halo, zn_ref[...])
    n_grp = len(POOL_WINDOWS)
    cg = d_pool // n_grp
    t_glob = it * tl + lax.broadcasted_iota(jnp.int32, (rows, cg), 0) // batch
    outs = []
    for gi, wnd in enumerate(POOL_WINDOWS):
        lo = wnd // 2
        hi = wnd - 1 - lo
        c0, c1 = gi * cg, (gi + 1) * cg
        acc = pad_scr[halo - lo * batch:halo - lo * batch + rows, c0:c1]
        for j in range(-lo + 1, hi + 1):
            acc = acc + pad_scr[halo + j * batch:halo + j * batch + rows, c0:c1]
        cnt = (jnp.clip(t_glob + hi + 1, 0, seq) - jnp.clip(t_glob - lo, 0, seq)).astype(F32)
        p = acc / cnt - z[:, c0:c1]
        outs.append(_dot(p.astype(BF16), pw_ref[gi]))
    ya = jnp.concatenate(outs, axis=-1) * ps_ref[...]

    m = _dot(jnp.concatenate([ya, yb], axis=-1).astype(BF16), wout_ref[...])
    _add_from_time_major(_rms(m, g_ref[...]), h_ref, o_ref, slab_scr)


def _ssm_discretise(a_re, a_im, log_dt, b_re, b_im, c_re, c_im):
    g, p = a_re.shape
    hch = b_re.shape[-1]
    a_re = jnp.minimum(a_re.astype(F32), A_RE_MAX)
    a_im = a_im.astype(F32)
    dt = jnp.exp(log_dt.astype(F32))[:, None]
    mag = jnp.exp(a_re * dt)
    lam_re = mag * jnp.cos(a_im * dt)
    lam_im = mag * jnp.sin(a_im * dt)
    num_re = lam_re - 1.0
    num_im = lam_im
    den = a_re * a_re + a_im * a_im
    f_re = ((num_re * a_re + num_im * a_im) / den)[..., None]
    f_im = ((num_im * a_re - num_re * a_im) / den)[..., None]
    b_re = b_re.astype(F32)
    b_im = b_im.astype(F32)
    bb_re = f_re * b_re - f_im * b_im
    bb_im = f_re * b_im + f_im * b_re
    gh = SSM_HALF_GROUPS
    n_half = g // gh
    w = gh * p
    eye = jnp.eye(gh, dtype=F32)

    def blockdiag_in(bb):
        bb = bb.reshape(n_half, gh, p, hch)
        return jnp.einsum("ngph,gk->nghkp", bb, eye).reshape(n_half, gh * hch, w)

    def blockdiag_out(c):
        c = c.reshape(n_half, gh, hch, p)
        return jnp.einsum("nghp,gk->ngpkh", c, eye).reshape(n_half, w, gh * hch)

    bmat = jnp.concatenate([blockdiag_in(bb_re), blockdiag_in(bb_im)], axis=-1).astype(BF16)
    cmat = jnp.concatenate([blockdiag_out(c_re.astype(F32)), -blockdiag_out(c_im.astype(F32))], axis=1).astype(BF16)
    lam = jnp.stack([lam_re.reshape(n_half, w), lam_im.reshape(n_half, w)], axis=1)
    lam = jnp.broadcast_to(lam[:, :, None, :], (n_half, 2, SUBLANES, w))
    return lam, bmat, cmat


def _mixer_even(h, g_in, g_out, w_in, pool_w, pool_scale, a_re, a_im, log_dt, b_re, b_im, c_re, c_im,
                d_skip, w_glu, b_glu, w_out, *, batch, seq):
    n, d = h.shape
    d_pool = pool_w.shape[0] * pool_w.shape[1]
    d_ssm = d - d_pool
    assert d_pool == d_ssm
    rows = ROW_TILE
    tl = rows // batch
    n_tiles = n // rows
    h3 = h.reshape(batch, seq, d)
    h_spec = pl.BlockSpec((batch, tl, d), lambda i: (0, i, 0))
    slab = pltpu.VMEM((d // LANES, rows, LANES), F32)

    z = pl.pallas_call(
        _s5_in_body, grid=(n_tiles,),
        in_specs=[h_spec, _const_spec((1, d)), _const_spec((d, d))],
        out_specs=_row_spec(d), out_shape=jax.ShapeDtypeStruct((n, d), F32),
        scratch_shapes=[slab], compiler_params=_params(1), name="s5_in",
    )(h3, g_in.reshape(1, d), w_in.astype(BF16))

    ssm = []
    for k in range(2):
        lam, bm, cm = _ssm_discretise(a_re[k], a_im[k], log_dt[k], b_re[k], b_im[k], c_re[k], c_im[k])
        ssm += [bm, lam, cm]
    n_half, kh, w2 = ssm[0].shape
    fwd = lambda i: (i, 0)
    bwd = lambda i: (n_tiles - 1 - i, 0)
    y_f, y_b = pl.pallas_call(
        _s5_scan_body, grid=(n_tiles,),
        in_specs=[pl.BlockSpec((rows, d_ssm), lambda i: (i, 1)),
                  pl.BlockSpec((rows, d_ssm), lambda i: (n_tiles - 1 - i, 1))] + [_const_spec(a.shape) for a in ssm],
        out_specs=[pl.BlockSpec((rows, d_ssm), fwd), pl.BlockSpec((rows, d_ssm), bwd)],
        out_shape=[jax.ShapeDtypeStruct((n, d_ssm), F32)] * 2,
        scratch_shapes=[pltpu.VMEM((2, n_half, 2, SUBLANES, w2 // 2), F32), pltpu.VMEM((2, n_half, rows, w2), F32),
                        pltpu.VMEM((2, n_half, rows, w2), BF16)],
        compiler_params=_params(1), name="s5_scan",
    )(z, z, *ssm)

    halo = max(POOL_WINDOWS) // 2 * batch
    hb = rows // halo
    n_hblk = n // halo
    out = pl.pallas_call(
        functools.partial(_s5_post_body, seq=seq), grid=(n_tiles,),
        in_specs=[h_spec, _row_spec(d),
                  pl.BlockSpec((halo, d_pool), lambda i: (jnp.maximum(i * hb - 1, 0), 0)),
                  pl.BlockSpec((halo, d_pool), lambda i: (jnp.minimum((i + 1) * hb, n_hblk - 1), 0)),
                  _row_spec(d_ssm), _row_spec(d_ssm), _const_spec((1, d)),
                  _const_spec((1, d_ssm)), _const_spec((d_ssm, d_ssm)), _const_spec((1, d_ssm)),
                  _const_spec(pool_w.shape), _const_spec((1, d_pool)), _const_spec((d, d))],
        out_specs=h_spec, out_shape=jax.ShapeDtypeStruct((batch, seq, d), F32),
        scratch_shapes=[slab, pltpu.VMEM((rows + 2 * halo, d_pool), F32)],
        compiler_params=_params(1), name="s5_post",
    )(h3, z, z, z, y_f, y_b, g_out.reshape(1, d), d_skip.reshape(1, d_ssm), w_glu.astype(BF16),
      b_glu.reshape(1, d_ssm), pool_w.astype(BF16), pool_scale.reshape(1, d_pool), w_out.astype(BF16))
    return out.reshape(n, d)


def _qkv_body(h_ref, g_ref, w_ref, o_ref, *, d_attn, q_scale):
    h = h_ref[...]
    inv = _inv_rms(h)
    hg = (h * g_ref[...]).astype(BF16)
    o_ref[:, :d_attn] = (_dot(hg, w_ref[:, :d_attn]) * (inv * q_scale)).astype(BF16)
    o_ref[:, d_attn:] = (_dot(hg, w_ref[:, d_attn:]) * inv).astype(BF16)


def _na_body(q_ref, k_ref, v_ref, bias_ref, o_ref, *, n_rows, kh, head_dim):
    gw = GRID_W
    lane = lax.broadcasted_iota(jnp.int32, (gw, 2 * head_dim), 1)
    first = lane < head_dim

    def step(j, _):
        rows, scores, probs = [], [], []
        for k in range(NA_ROW_UNROLL):
            r = j * NA_ROW_UNROLL + k
            r0 = jnp.clip(r - kh // 2, 0, n_rows - kh)
            qrows = pl.ds(pl.multiple_of(r * gw, gw), gw)
            krows = pl.ds(pl.multiple_of(r0 * gw, gw), kh * gw)
            q = q_ref[qrows, :]
            zero = jnp.zeros_like(q)
            q2 = jnp.concatenate([jnp.where(first, q, zero), jnp.where(first, zero, q)], axis=0)
            s = lax.dot_general(q2, k_ref[krows, :], (((1,), (1,)), ((), ())), preferred_element_type=F32)
            rows.append((qrows, krows))
            scores.append(s + bias_ref[r0 - r + (MAX_KH - 1)])
        for s in scores:
            e = jnp.exp(s - jnp.max(s, axis=-1, keepdims=True))
            probs.append((e.astype(BF16), jnp.sum(e, axis=-1, keepdims=True)))
        for (qrows, krows), (p, l) in zip(rows, probs):
            o2 = _dot(p, v_ref[krows, :]) / l
            o_ref[qrows, :] = jnp.where(first, o2[:gw], o2[gw:]).astype(o_ref.dtype)
        return 0

    lax.fori_loop(0, n_rows // NA_ROW_UNROLL, step, 0)


def _na_bias_tables(rpb, kh):
    n_heads = rpb.shape[0]
    col = jnp.arange(GRID_W)
    col_start = jnp.clip(col - KW // 2, 0, GRID_W - KW)
    kc = jnp.arange(GRID_W)
    rel = kc[None, :] - col[:, None] + (KW - 1)
    valid = (kc[None, :] >= col_start[:, None]) & (kc[None, :] < col_start[:, None] + KW)
    rel = jnp.clip(rel, 0, 2 * KW - 2)
    t = rpb.astype(F32)[:, :, rel]
    t = jnp.where(valid[None, None], t, MASK_VALUE)
    offs = jnp.arange(MAX_KH)[:, None] + jnp.arange(kh)[None, :]
    tab = t[:, offs]
    tab = jnp.transpose(tab, (0, 1, 3, 2, 4)).reshape(n_heads, MAX_KH, GRID_W, kh * GRID_W)
    tab = tab.reshape(n_heads // 2, 2, MAX_KH, GRID_W, kh * GRID_W)
    return jnp.transpose(tab, (0, 2, 1, 3, 4)).reshape(n_heads // 2, MAX_KH, 2 * GRID_W, kh * GRID_W)


def _mixer_odd(h, g_in, w_qkv, rpb, *, batch, seq):
    n, d = h.shape
    d_attn = w_qkv.shape[1] // 3
    head_dim = d_attn // N_HEADS
    n_tiles = n // ROW_TILE
    n_rows = seq // GRID_W
    kh = min(MAX_KH, n_rows)
    hp_lanes = 2 * head_dim
    n_hp = d_attn // hp_lanes

    qkv = pl.pallas_call(
        functools.partial(_qkv_body, d_attn=d_attn, q_scale=head_dim ** -0.5), grid=(n_tiles,),
        in_specs=[_row_spec(d), _const_spec((1, d)), _const_spec((d, 3 * d_attn))],
        out_specs=_row_spec(3 * d_attn), out_shape=jax.ShapeDtypeStruct((n, 3 * d_attn), BF16),
        compiler_params=_params(1), name="na_qkv",
    )(h, g_in.reshape(1, d), w_qkv.astype(BF16))

    bias = _na_bias_tables(rpb, kh)
    attn = pl.pallas_call(
        functools.partial(_na_body, n_rows=n_rows, kh=kh, head_dim=head_dim), grid=(n_hp, batch),
        in_specs=[pl.BlockSpec((seq, hp_lanes), lambda p, b: (b, p)),
                  pl.BlockSpec((seq, hp_lanes), lambda p, b: (b, n_hp + p)),
                  pl.BlockSpec((seq, hp_lanes), lambda p, b: (b, 2 * n_hp + p)),
                  pl.BlockSpec((None, MAX_KH, 2 * GRID_W, kh * GRID_W), lambda p, b: (p, 0, 0, 0))],
        out_specs=pl.BlockSpec((seq, hp_lanes), lambda p, b: (b, p)),
        out_shape=jax.ShapeDtypeStruct((n, d_attn), BF16),
        compiler_params=_params(2), name="na_attn",
    )(qkv, qkv, qkv, bias)
    return attn


def kernel(x, norm_g, ffn_w_gate, ffn_w_up, ffn_w_down, ab_w_in, pool_w, pool_scale, ssm_A_re, ssm_A_im, ssm_log_dt, ssm_B_re, ssm_B_im, ssm_C_re, ssm_C_im, ssm_D, ssm_w_glu, ssm_b_glu, ab_w_out, na_w_qkv, na_rpb, na_w_out):
    batch, seq, d = x.shape
    depth = norm_g.shape[0]
    n_rows = seq // GRID_W
    assert batch == SUBLANES and seq % ROW_TILE == 0 and n_rows % NA_ROW_UNROLL == 0

    def ffn(h, layer, which, **attn_args):
        g = norm_g[layer]
        return _ffn_call(h, g[0:2] if which == 0 else g[4:6], ffn_w_gate[layer, which].astype(BF16),
                         ffn_w_up[layer, which].astype(BF16), ffn_w_down[layer, which].astype(BF16), **attn_args)

    h = x.reshape(batch * seq, d)
    for layer in range(depth):
        g = norm_g[layer]
        h = ffn(h, layer, 0)
        i = layer // 2
        if layer % 2 == 0:
            h = _mixer_even(h, g[2], g[3], ab_w_in[i], pool_w[i], pool_scale[i], ssm_A_re[i], ssm_A_im[i],
                            ssm_log_dt[i], ssm_B_re[i], ssm_B_im[i], ssm_C_re[i], ssm_C_im[i], ssm_D[i],
                            ssm_w_glu[i], ssm_b_glu[i], ab_w_out[i], batch=batch, seq=seq)
            h = ffn(h, layer, 1)
        else:
            attn = _mixer_odd(h, g[2], na_w_qkv[i], na_rpb[i], batch=batch, seq=seq)
            h = ffn(h, layer, 1, attn=attn, g_attn=g[3], w_attn=na_w_out[i].astype(BF16))
    return h.reshape(batch, seq, d)
```

```python
import functools

import jax
import jax.numpy as jnp
from jax import lax
from jax.experimental import pallas as pl
from jax.experimental.pallas import tpu as pltpu

F32 = jnp.float32
BF16 = jnp.bfloat16

RMS_EPS = 1e-6
A_RE_MAX = -1e-4
GRID_W = 64
POOL_WINDOWS = (2, 4, 8, 16)
N_HEADS = 16
MAX_KH = 8
KW = 16
MASK_VALUE = -1e30

SUBLANES = 8
LANES = 128
BF16_SUBLANES = 16
VMEM_LIMIT_BYTES = 56 * 1024 * 1024

ROW_TILE = 512
SSM_HALF_GROUPS = 16
NA_ROW_UNROLL = 8
OUT_CHUNK = 256
OUT_ROW_GROUPS = 2


def _rms(x, g):
    return x * lax.rsqrt(jnp.mean(x * x, axis=-1, keepdims=True) + RMS_EPS) * g


def _dot(a, b):
    return jnp.dot(a, b, preferred_element_type=F32)


def _const_spec(shape):
    nd = len(shape)
    return pl.BlockSpec(shape, lambda *_: (0,) * nd, pipeline_mode=pl.Buffered(1))


def _params(n_axes):
    return pltpu.CompilerParams(
        dimension_semantics=("arbitrary",) * n_axes, vmem_limit_bytes=VMEM_LIMIT_BYTES)


def _row_spec(width):
    return pl.BlockSpec((ROW_TILE, width), lambda i: (i, 0))


def _inv_rms(x):
    return lax.rsqrt(jnp.mean(x * x, axis=-1, keepdims=True) + RMS_EPS)


def _normed_dot(x, g, inv, w):
    return _dot((x * g).astype(BF16), w) * inv


FFN_VARIANTS = {
    "A": dict(post=False, cols=False, rgroups=1, ff_chunks=1, tile=512),
    "B": dict(post=True, cols=False, rgroups=1, ff_chunks=1, tile=512),
    "C": dict(post=True, cols=True, rgroups=1, ff_chunks=1, tile=512),
    "D": dict(post=True, cols=True, rgroups=2, ff_chunks=1, tile=512),
    "E": dict(post=False, cols=True, rgroups=1, ff_chunks=1, tile=512),
    "F": dict(post=False, cols=False, rgroups=1, ff_chunks=3, tile=1024),
}


def _ffn_residual(x, g_ref, wg_ref, wu_ref, wd_ref, o_ref, v):
    d = x.shape[1]
    ff = wg_ref.shape[1]
    if v["post"]:
        xg = (x * g_ref[0:1, :]).astype(BF16)
        inv = _inv_rms(x)
    else:
        xg = _rms(x, g_ref[0:1, :]).astype(BF16)
        inv = None

    def hidden(c0, c1):
        gate = _dot(xg, wg_ref[:, c0:c1])
        up = _dot(xg, wu_ref[:, c0:c1])
        if inv is not None:
            gate, up = gate * inv, up * inv
        return (gate * jax.nn.sigmoid(gate) * up).astype(BF16)

    if v["ff_chunks"] > 1:
        step = -(-ff // v["ff_chunks"] // 256) * 256
        f = None
        for c0 in range(0, ff, step):
            c1 = min(c0 + step, ff)
            part = _dot(hidden(c0, c1), wd_ref[c0:c1, :])
            f = part if f is None else f + part
        o_ref[...] = x + 0.5 * _rms(f, g_ref[1:2, :])
        return
    act = hidden(0, ff)
    if not v["cols"]:
        o_ref[...] = x + 0.5 * _rms(_dot(act, wd_ref[...]), g_ref[1:2, :])
        return
    rg = x.shape[0] // v["rgroups"]
    for r in range(v["rgroups"]):
        rows = slice(r * rg, (r + 1) * rg)
        chunks, ssq = [], None
        for c in range(0, d, OUT_CHUNK):
            f = _dot(act[rows], wd_ref[:, c:c + OUT_CHUNK])
            chunks.append(f)
            sq = jnp.sum(f * f, axis=-1, keepdims=True)
            ssq = sq if ssq is None else ssq + sq
        half_inv = 0.5 * lax.rsqrt(ssq / d + RMS_EPS)
        for k, f in enumerate(chunks):
            cols = slice(k * OUT_CHUNK, (k + 1) * OUT_CHUNK)
            o_ref[rows, cols] = x[rows, cols] + f * half_inv * g_ref[1:2, cols]


def _ffn_body(x_ref, g_ref, wg_ref, wu_ref, wd_ref, o_ref, *, v):
    _ffn_residual(x_ref[...], g_ref, wg_ref, wu_ref, wd_ref, o_ref, v)


def _attn_out_ffn_body(a_ref, x_ref, go_ref, wo_ref, g_ref, wg_ref, wu_ref, wd_ref, o_ref, *, v):
    x = x_ref[...] + _rms(_dot(a_ref[...], wo_ref[...]), go_ref[...])
    _ffn_residual(x, g_ref, wg_ref, wu_ref, wd_ref, o_ref, v)


def _ffn_call(x, g_pair, wg, wu, wd, variant, attn=None, g_attn=None, w_attn=None):
    n, d = x.shape
    f = wg.shape[1]
    v = FFN_VARIANTS[variant]
    tile = v["tile"]
    row_spec = lambda width: pl.BlockSpec((tile, width), lambda i: (i, 0))
    ffn_specs = [_const_spec((2, d)), _const_spec((d, f)), _const_spec((d, f)), _const_spec((f, d))]
    if attn is None:
        body, name, pre_specs, pre_args = _ffn_body, "ffn_" + variant, [row_spec(d)], (x,)
    else:
        da = attn.shape[1]
        body, name = _attn_out_ffn_body, "attn_out_ffn_" + variant
        pre_specs = [row_spec(da), row_spec(d), _const_spec((1, d)), _const_spec((da, d))]
        pre_args = (attn, x, g_attn.reshape(1, d), w_attn)
    return pl.pallas_call(
        functools.partial(body, v=v), grid=(n // tile,), in_specs=pre_specs + ffn_specs,
        out_specs=row_spec(d), out_shape=jax.ShapeDtypeStruct((n, d), F32),
        compiler_params=_params(1), name=name,
    )(*pre_args, g_pair, wg, wu, wd)


def _to_time_major(src_ref, slab_scr):
    nb, tl, d = src_ref.shape
    for b in range(nb):
        for s in range(d // LANES):
            slab_scr[s, pl.ds(b, tl, stride=nb), :] = src_ref[b, :, s * LANES:(s + 1) * LANES]
    return jnp.concatenate([slab_scr[s] for s in range(d // LANES)], axis=-1)


def _add_from_time_major(val, res_ref, dst_ref, slab_scr):
    nb, tl, d = dst_ref.shape
    for s in range(d // LANES):
        slab_scr[s] = val[:, s * LANES:(s + 1) * LANES]
    for b in range(nb):
        for s in range(d // LANES):
            cols = slice(s * LANES, (s + 1) * LANES)
            dst_ref[b, :, cols] = res_ref[b, :, cols] + slab_scr[s, pl.ds(b, tl, stride=nb), :]


def _scan_half(x_scr, s_scr, lam_ref, st_ref, half, reverse):
    rows, w2 = x_scr.shape[1], x_scr.shape[2]
    w = w2 // 2
    n_steps = rows // SUBLANES
    lr = lam_ref[half, 0]
    li = lam_ref[half, 1]
    sr = st_ref[half, 0]
    si = st_ref[half, 1]
    order = range(n_steps - 1, -1, -1) if reverse else range(n_steps)
    held = None
    for t in order:
        r0 = t * SUBLANES
        xr = x_scr[half, r0:r0 + SUBLANES, 0:w]
        xi = x_scr[half, r0:r0 + SUBLANES, w:w2]
        sr, si = lr * sr - li * si + xr, lr * si + li * sr + xi
        if held is None:
            held = (sr, si)
        else:
            pr, pi = held
            lo = min(t, t + (1 if reverse else -1)) * SUBLANES
            pair_r = (sr, pr) if reverse else (pr, sr)
            pair_i = (si, pi) if reverse else (pi, si)
            s_scr[half, lo:lo + BF16_SUBLANES, 0:w] = jnp.concatenate(pair_r, axis=0).astype(BF16)
            s_scr[half, lo:lo + BF16_SUBLANES, w:w2] = jnp.concatenate(pair_i, axis=0).astype(BF16)
            held = None
    st_ref[half, 0] = sr
    st_ref[half, 1] = si


def _s5_in_body(h_ref, g_ref, win_ref, z_ref, slab_scr):
    h = _to_time_major(h_ref, slab_scr)
    z_ref[...] = _normed_dot(h, g_ref[...], _inv_rms(h), win_ref[...])


def _s5_scan_body(uf_ref, ub_ref, bf_ref, lamf_ref, cf_ref, bb_ref, lamb_ref, cb_ref, yf_ref, yb_ref,
                  st_ref, x_scr, s_scr):
    @pl.when(pl.program_id(0) == 0)
    def _():
        st_ref[...] = jnp.zeros_like(st_ref)

    dirs = ((uf_ref, bf_ref, lamf_ref, cf_ref, yf_ref, False), (ub_ref, bb_ref, lamb_ref, cb_ref, yb_ref, True))
    n_half, kh = bf_ref.shape[0], bf_ref.shape[1]
    for k, (u_ref, b_ref, _, _, _, _) in enumerate(dirs):
        u_bf = u_ref[...].astype(BF16)
        for half in range(n_half):
            x_scr[k, half] = _dot(u_bf[:, half * kh:(half + 1) * kh], b_ref[half])
    for k, (_, _, lam_ref, c_ref, y_ref, reverse) in enumerate(dirs):
        for half in range(n_half):
            _scan_half(x_scr.at[k], s_scr.at[k], lam_ref, st_ref.at[k], half, reverse)
            y_ref[:, half * kh:(half + 1) * kh] = _dot(s_scr[k, half], c_ref[half])


def _s5_post_body(h_ref, z_ref, zp_ref, zn_ref, yf_ref, yb_ref, g_ref, dsk_ref, wglu_ref, bglu_ref,
                  pw_ref, ps_ref, wout_ref, o_ref, slab_scr, pad_scr, *, seq):
    it = pl.program_id(0)
    n_tiles = pl.num_programs(0)
    batch = h_ref.shape[0]
    rows = z_ref.shape[0]
    d_pool = pad_scr.shape[1]
    halo = zp_ref.shape[0]
    tl = rows // batch
    z = z_ref[...]

    u = z[:, d_pool:]
    y = yf_ref[...] + yb_ref[...] + dsk_ref[...] * u
    y = jax.nn.gelu(y, approximate=True)
    yb = y * jax.nn.sigmoid(_dot(y.astype(BF16), wglu_ref[...]) + bglu_ref[...])

    zero_halo = jnp.zeros((halo, d_pool), F32)
    pad_scr[0:halo, :] = jnp.where(it == 0, zero_halo, zp_ref[...])
    pad_scr[halo:halo + rows, :] = z[:, :d_pool]
    pad_scr[halo + rows:, :] = jnp.where(it == n_tiles - 1, zero_halo, zn_ref[...])
    n_grp = len(POOL_WINDOWS)
    cg = d_pool // n_grp
    t_glob = it * tl + lax.broadcasted_iota(jnp.int32, (rows, cg), 0) // batch
    outs = []
    for gi, wnd in enumerate(POOL_WINDOWS):
        lo = wnd // 2
        hi = wnd - 1 - lo
        c0, c1 = gi * cg, (gi + 1) * cg
        acc = pad_scr[halo - lo * batch:halo - lo * batch + rows, c0:c1]
        for j in range(-lo + 1, hi + 1):
            acc = acc + pad_scr[halo + j * batch:halo + j * batch + rows, c0:c1]
        cnt = (jnp.clip(t_glob + hi + 1, 0, seq) - jnp.clip(t_glob - lo, 0, seq)).astype(F32)
        p = acc / cnt - z[:, c0:c1]
        outs.append(_dot(p.astype(BF16), pw_ref[gi]))
    ya = jnp.concatenate(outs, axis=-1) * ps_ref[...]

    m = _dot(jnp.concatenate([ya, yb], axis=-1).astype(BF16), wout_ref[...])
    _add_from_time_major(_rms(m, g_ref[...]), h_ref, o_ref, slab_scr)


def _ssm_discretise(a_re, a_im, log_dt, b_re, b_im, c_re, c_im):
    g, p = a_re.shape
    hch = b_re.shape[-1]
    a_re = jnp.minimum(a_re.astype(F32), A_RE_MAX)
    a_im = a_im.astype(F32)
    dt = jnp.exp(log_dt.astype(F32))[:, None]
    mag = jnp.exp(a_re * dt)
    lam_re = mag * jnp.cos(a_im * dt)
    lam_im = mag * jnp.sin(a_im * dt)
    num_re = lam_re - 1.0
    num_im = lam_im
    den = a_re * a_re + a_im * a_im
    f_re = ((num_re * a_re + num_im * a_im) / den)[..., None]
    f_im = ((num_im * a_re - num_re * a_im) / den)[..., None]
    b_re = b_re.astype(F32)
    b_im = b_im.astype(F32)
    bb_re = f_re * b_re - f_im * b_im
    bb_im = f_re * b_im + f_im * b_re
    gh = SSM_HALF_GROUPS
    n_half = g // gh
    w = gh * p
    eye = jnp.eye(gh, dtype=F32)

    def blockdiag_in(bb):
        bb = bb.reshape(n_half, gh, p, hch)
        return jnp.einsum("ngph,gk->nghkp", bb, eye).reshape(n_half, gh * hch, w)

    def blockdiag_out(c):
        c = c.reshape(n_half, gh, hch, p)
        return jnp.einsum("nghp,gk->ngpkh", c, eye).reshape(n_half, w, gh * hch)

    bmat = jnp.concatenate([blockdiag_in(bb_re), blockdiag_in(bb_im)], axis=-1).astype(BF16)
    cmat = jnp.concatenate([blockdiag_out(c_re.astype(F32)), -blockdiag_out(c_im.astype(F32))], axis=1).astype(BF16)
    lam = jnp.stack([lam_re.reshape(n_half, w), lam_im.reshape(n_half, w)], axis=1)
    lam = jnp.broadcast_to(lam[:, :, None, :], (n_half, 2, SUBLANES, w))
    return lam, bmat, cmat


def _mixer_even(h, g_in, g_out, w_in, pool_w, pool_scale, a_re, a_im, log_dt, b_re, b_im, c_re, c_im,
                d_skip, w_glu, b_glu, w_out, *, batch, seq):
    n, d = h.shape
    d_pool = pool_w.shape[0] * pool_w.shape[1]
    d_ssm = d - d_pool
    assert d_pool == d_ssm
    rows = ROW_TILE
    tl = rows // batch
    n_tiles = n // rows
    h3 = h.reshape(batch, seq, d)
    h_spec = pl.BlockSpec((batch, tl, d), lambda i: (0, i, 0))
    slab = pltpu.VMEM((d // LANES, rows, LANES), F32)

    z = pl.pallas_call(
        _s5_in_body, grid=(n_tiles,),
        in_specs=[h_spec, _const_spec((1, d)), _const_spec((d, d))],
        out_specs=_row_spec(d), out_shape=jax.ShapeDtypeStruct((n, d), F32),
        scratch_shapes=[slab], compiler_params=_params(1), name="s5_in",
    )(h3, g_in.reshape(1, d), w_in.astype(BF16))

    ssm = []
    for k in range(2):
        lam, bm, cm = _ssm_discretise(a_re[k], a_im[k], log_dt[k], b_re[k], b_im[k], c_re[k], c_im[k])
        ssm += [bm, lam, cm]
    n_half, kh, w2 = ssm[0].shape
    fwd = lambda i: (i, 0)
    bwd = lambda i: (n_tiles - 1 - i, 0)
    y_f, y_b = pl.pallas_call(
        _s5_scan_body, grid=(n_tiles,),
        in_specs=[pl.BlockSpec((rows, d_ssm), lambda i: (i, 1)),
                  pl.BlockSpec((rows, d_ssm), lambda i: (n_tiles - 1 - i, 1))] + [_const_spec(a.shape) for a in ssm],
        out_specs=[pl.BlockSpec((rows, d_ssm), fwd), pl.BlockSpec((rows, d_ssm), bwd)],
        out_shape=[jax.ShapeDtypeStruct((n, d_ssm), F32)] * 2,
        scratch_shapes=[pltpu.VMEM((2, n_half, 2, SUBLANES, w2 // 2), F32), pltpu.VMEM((2, n_half, rows, w2), F32),
                        pltpu.VMEM((2, n_half, rows, w2), BF16)],
        compiler_params=_params(1), name="s5_scan",
    )(z, z, *ssm)

    halo = max(POOL_WINDOWS) // 2 * batch
    hb = rows // halo
    n_hblk = n // halo
    out = pl.pallas_call(
        functools.partial(_s5_post_body, seq=seq), grid=(n_tiles,),
        in_specs=[h_spec, _row_spec(d),
                  pl.BlockSpec((halo, d_pool), lambda i: (jnp.maximum(i * hb - 1, 0), 0)),
                  pl.BlockSpec((halo, d_pool), lambda i: (jnp.minimum((i + 1) * hb, n_hblk - 1), 0)),
                  _row_spec(d_ssm), _row_spec(d_ssm), _const_spec((1, d)),
                  _const_spec((1, d_ssm)), _const_spec((d_ssm, d_ssm)), _const_spec((1, d_ssm)),
                  _const_spec(pool_w.shape), _const_spec((1, d_pool)), _const_spec((d, d))],
        out_specs=h_spec, out_shape=jax.ShapeDtypeStruct((batch, seq, d), F32),
        scratch_shapes=[slab, pltpu.VMEM((rows + 2 * halo, d_pool), F32)],
        compiler_params=_params(1), name="s5_post",
    )(h3, z, z, z, y_f, y_b, g_out.reshape(1, d), d_skip.reshape(1, d_ssm), w_glu.astype(BF16),
      b_glu.reshape(1, d_ssm), pool_w.astype(BF16), pool_scale.reshape(1, d_pool), w_out.astype(BF16))
    return out.reshape(n, d)


def _qkv_body(h_ref, g_ref, w_ref, o_ref, *, d_attn, q_scale):
    h = h_ref[...]
    inv = _inv_rms(h)
    hg = (h * g_ref[...]).astype(BF16)
    o_ref[:, :d_attn] = (_dot(hg, w_ref[:, :d_attn]) * (inv * q_scale)).astype(BF16)
    o_ref[:, d_attn:] = (_dot(hg, w_ref[:, d_attn:]) * inv).astype(BF16)


def _na_body(q_ref, k_ref, v_ref, bias_ref, o_ref, *, n_rows, kh, head_dim, unroll):
    gw = GRID_W
    lane = lax.broadcasted_iota(jnp.int32, (gw, 2 * head_dim), 1)
    first = lane < head_dim

    def step(j, _):
        rows, scores, probs = [], [], []
        for k in range(unroll):
            r = j * unroll + k
            r0 = jnp.clip(r - kh // 2, 0, n_rows - kh)
            qrows = pl.ds(pl.multiple_of(r * gw, gw), gw)
            krows = pl.ds(pl.multiple_of(r0 * gw, gw), kh * gw)
            q = q_ref[qrows, :]
            zero = jnp.zeros_like(q)
            q2 = jnp.concatenate([jnp.where(first, q, zero), jnp.where(first, zero, q)], axis=0)
            s = lax.dot_general(q2, k_ref[krows, :], (((1,), (1,)), ((), ())), preferred_element_type=F32)
            rows.append((qrows, krows))
            scores.append(s + bias_ref[r0 - r + (MAX_KH - 1)])
        for s in scores:
            e = jnp.exp(s - jnp.max(s, axis=-1, keepdims=True))
            probs.append((e.astype(BF16), jnp.sum(e, axis=-1, keepdims=True)))
        for (qrows, krows), (p, l) in zip(rows, probs):
            o2 = _dot(p, v_ref[krows, :]) / l
            o_ref[qrows, :] = jnp.where(first, o2[:gw], o2[gw:]).astype(o_ref.dtype)
        return 0

    lax.fori_loop(0, n_rows // unroll, step, 0)


def _na_bias_tables(rpb, kh):
    n_heads = rpb.shape[0]
    col = jnp.arange(GRID_W)
    col_start = jnp.clip(col - KW // 2, 0, GRID_W - KW)
    kc = jnp.arange(GRID_W)
    rel = kc[None, :] - col[:, None] + (KW - 1)
    valid = (kc[None, :] >= col_start[:, None]) & (kc[None, :] < col_start[:, None] + KW)
    rel = jnp.clip(rel, 0, 2 * KW - 2)
    t = rpb.astype(F32)[:, :, rel]
    t = jnp.where(valid[None, None], t, MASK_VALUE)
    offs = jnp.arange(MAX_KH)[:, None] + jnp.arange(kh)[None, :]
    tab = t[:, offs]
    tab = jnp.transpose(tab, (0, 1, 3, 2, 4)).reshape(n_heads, MAX_KH, GRID_W, kh * GRID_W)
    tab = tab.reshape(n_heads // 2, 2, MAX_KH, GRID_W, kh * GRID_W)
    return jnp.transpose(tab, (0, 2, 1, 3, 4)).reshape(n_heads // 2, MAX_KH, 2 * GRID_W, kh * GRID_W)


def _mixer_odd(h, g_in, w_qkv, rpb, *, batch, seq, unroll):
    n, d = h.shape
    d_attn = w_qkv.shape[1] // 3
    head_dim = d_attn // N_HEADS
    n_tiles = n // ROW_TILE
    n_rows = seq // GRID_W
    kh = min(MAX_KH, n_rows)
    hp_lanes = 2 * head_dim
    n_hp = d_attn // hp_lanes

    qkv = pl.pallas_call(
        functools.partial(_qkv_body, d_attn=d_attn, q_scale=head_dim ** -0.5), grid=(n_tiles,),
        in_specs=[_row_spec(d), _const_spec((1, d)), _const_spec((d, 3 * d_attn))],
        out_specs=_row_spec(3 * d_attn), out_shape=jax.ShapeDtypeStruct((n, 3 * d_attn), BF16),
        compiler_params=_params(1), name="na_qkv",
    )(h, g_in.reshape(1, d), w_qkv.astype(BF16))

    bias = _na_bias_tables(rpb, kh)
    attn = pl.pallas_call(
        functools.partial(_na_body, n_rows=n_rows, kh=kh, head_dim=head_dim, unroll=unroll), grid=(n_hp, batch),
        in_specs=[pl.BlockSpec((seq, hp_lanes), lambda p, b: (b, p)),
                  pl.BlockSpec((seq, hp_lanes), lambda p, b: (b, n_hp + p)),
                  pl.BlockSpec((seq, hp_lanes), lambda p, b: (b, 2 * n_hp + p)),
                  pl.BlockSpec((None, MAX_KH, 2 * GRID_W, kh * GRID_W), lambda p, b: (p, 0, 0, 0))],
        out_specs=pl.BlockSpec((seq, hp_lanes), lambda p, b: (b, p)),
        out_shape=jax.ShapeDtypeStruct((n, d_attn), BF16),
        compiler_params=_params(2), name="na_attn_u%d" % unroll,
    )(qkv, qkv, qkv, bias)
    return attn


def kernel(x, norm_g, ffn_w_gate, ffn_w_up, ffn_w_down, ab_w_in, pool_w, pool_scale, ssm_A_re, ssm_A_im, ssm_log_dt, ssm_B_re, ssm_B_im, ssm_C_re, ssm_C_im, ssm_D, ssm_w_glu, ssm_b_glu, ab_w_out, na_w_qkv, na_rpb, na_w_out):
    batch, seq, d = x.shape
    depth = norm_g.shape[0]
    n_rows = seq // GRID_W
    assert batch == SUBLANES and seq % ROW_TILE == 0 and n_rows % NA_ROW_UNROLL == 0

    def ffn(h, layer, which, variant, **attn_args):
        g = norm_g[layer]
        return _ffn_call(h, g[0:2] if which == 0 else g[4:6], ffn_w_gate[layer, which].astype(BF16),
                         ffn_w_up[layer, which].astype(BF16), ffn_w_down[layer, which].astype(BF16), variant,
                         **attn_args)

    first_variant = ("A", "C", "D", "F")
    second_variant = ("B", "F", "E", "A")
    na_unroll = (8, 4)
    h = x.reshape(batch * seq, d)
    for layer in range(depth):
        g = norm_g[layer]
        h = ffn(h, layer, 0, first_variant[layer % 4])
        i = layer // 2
        if layer % 2 == 0:
            h = _mixer_even(h, g[2], g[3], ab_w_in[i], pool_w[i], pool_scale[i], ssm_A_re[i], ssm_A_im[i],
                            ssm_log_dt[i], ssm_B_re[i], ssm_B_im[i], ssm_C_re[i], ssm_C_im[i], ssm_D[i],
                            ssm_w_glu[i], ssm_b_glu[i], ab_w_out[i], batch=batch, seq=seq)
            h = ffn(h, layer, 1, second_variant[layer % 4])
        else:
            attn = _mixer_odd(h, g[2], na_w_qkv[i], na_rpb[i], batch=batch, seq=seq, unroll=na_unroll[i % 2])
            h = ffn(h, layer, 1, second_variant[layer % 4], attn=attn, g_attn=g[3],
                    w_attn=na_w_out[i].astype(BF16))
    return h.reshape(batch, seq, d)
```
